```python
import jax, jax.numpy as jnp
from jax import lax
import numpy as np

D_MODEL = 1024
BATCH = 32
SEQ = 2048
DEPTH = 1
DEC_BATCH = 8
DEC_SEQ = 4096
PAST_LEN = 128

M_HEADS = 4
M_HEAD_DIM = 128
M_WIDTH = M_HEADS * M_HEAD_DIM
M_CHUNK = 64
N_GATE = 4 * M_HEADS
A_HEADS = 8
A_KV_HEADS = 2
A_GROUP = A_HEADS // A_KV_HEADS
A_HEAD_DIM = 64
A_WIDTH = A_HEADS * A_HEAD_DIM
A_KV_WIDTH = A_KV_HEADS * A_HEAD_DIM
WINDOW = 128
A_BLOCK = WINDOW
ROPE_THETA = 10000.0
D_FF = 2816
IN_COLS = 4 * M_WIDTH + N_GATE + A_WIDTH + 2 * A_KV_WIDTH + 2 * D_MODEL
EPS = 1e-6
NEG = -1e30

kernel_name = 'hybrid_mlstm_swa_encoder'


def rms_norm(x, g):
    xf = x.astype(jnp.float32)
    xf = xf * lax.rsqrt(jnp.mean(xf * xf, axis=-1, keepdims=True) + EPS)
    return xf.astype(x.dtype) * g


def swiglu(x, w1, w3, w2):
    return (jax.nn.silu(x @ w1) * (x @ w3)) @ w2


def rope(x, pos):
    half = x.shape[-1] // 2
    inv = jnp.power(ROPE_THETA, -jnp.arange(half, dtype=jnp.float32) / half)
    ang = pos.astype(jnp.float32)[:, None] * inv[None, :]
    cos = jnp.cos(ang)[:, None, :]
    sin = jnp.sin(ang)[:, None, :]
    xf = x.astype(jnp.float32)
    x1, x2 = xf[..., :half], xf[..., half:]
    return jnp.concatenate([x1 * cos - x2 * sin, x2 * cos + x1 * sin], axis=-1).astype(x.dtype)


def mlstm_dir(q, k, v, log_i, log_f):
    B, H, S, Dh = q.shape
    L = M_CHUNK
    NC = S // L
    qc = q.reshape(B, H, NC, L, Dh)
    kc = k.reshape(B, H, NC, L, Dh)
    vc = v.reshape(B, H, NC, L, Dh)
    li = log_i.reshape(B, H, NC, L)
    bcum = jnp.cumsum(log_f.reshape(B, H, NC, L), axis=-1)
    g = bcum[..., -1]
    a = g[..., None] - bcum + li
    m_loc = jnp.max(a, axis=-1)
    w = jnp.exp(a - m_loc[..., None])
    dC = jnp.einsum('bhcsk,bhcsv->bhckv', kc * w[..., None], vc)
    dn = jnp.einsum('bhcsk,bhcs->bhck', kc, w)

    def step(carry, inp):
        C, n, m = carry
        dC_c, dn_c, g_c, ml_c = inp
        m_new = jnp.maximum(g_c + m, ml_c)
        s_old = jnp.exp(g_c + m - m_new)
        s_new = jnp.exp(ml_c - m_new)
        C_new = s_old[..., None, None] * C + s_new[..., None, None] * dC_c
        n_new = s_old[..., None] * n + s_new[..., None] * dn_c
        return (C_new, n_new, m_new), (C, n, m)

    init = (jnp.zeros((B, H, Dh, Dh), jnp.float32),
            jnp.zeros((B, H, Dh), jnp.float32),
            jnp.full((B, H), NEG, jnp.float32))
    xs = (jnp.moveaxis(dC, 2, 0), jnp.moveaxis(dn, 2, 0),
          jnp.moveaxis(g, 2, 0), jnp.moveaxis(m_loc, 2, 0))
    _, (Cp, npv, mp) = lax.scan(step, init, xs)
    Cp = jnp.moveaxis(Cp, 0, 2)
    npv = jnp.moveaxis(npv, 0, 2)
    mp = jnp.moveaxis(mp, 0, 2)

    causal = jnp.tril(jnp.ones((L, L), dtype=bool))
    dmat = bcum[..., :, None] - bcum[..., None, :] + li[..., None, :]
    dmat = jnp.where(causal, dmat, NEG)
    m_inter = bcum + mp[..., None]
    m_t = jnp.maximum(jnp.max(dmat, axis=-1), m_inter)
    p = jnp.exp(dmat - m_t[..., None])
    wq = p * jnp.einsum('bhctd,bhcsd->bhcts', qc, kc)
    s_inter = jnp.exp(m_inter - m_t)
    num = (jnp.einsum('bhcts,bhcsv->bhctv', wq, vc)
           + s_inter[..., None] * jnp.einsum('bhctk,bhckv->bhctv', qc, Cp))
    den = jnp.sum(wq, axis=-1) + s_inter * jnp.einsum('bhctk,bhck->bhct', qc, npv)
    h = num / jnp.maximum(jnp.abs(den), jnp.exp(-m_t))[..., None]
    return h.reshape(B, H, S, Dh)


def mlstm_branch(mq, mk, mv, mo, mg, b_gates, m_norm):
    B, S, _ = mq.shape

    def heads(t):
        return t.reshape(B, S, M_HEADS, M_HEAD_DIM).transpose(0, 2, 1, 3).astype(jnp.float32)

    qh = heads(mq)
    kh = heads(mk) * (M_HEAD_DIM ** -0.5)
    vh = heads(mv)
    gt = (mg + b_gates).astype(jnp.float32).reshape(B, S, 4, M_HEADS).transpose(2, 0, 3, 1)
    i_f, f_f, i_b, f_b = gt[0], gt[1], gt[2], gt[3]
    h_fwd = mlstm_dir(qh, kh, vh, i_f, jax.nn.log_sigmoid(f_f))
    flip = lambda t: jnp.flip(t, axis=2)
    h_bwd = flip(mlstm_dir(flip(qh), flip(kh), flip(vh), flip(i_b), flip(jax.nn.log_sigmoid(f_b))))
    h = h_fwd + h_bwd
    h = h * lax.rsqrt(jnp.mean(h * h, axis=-1, keepdims=True) + EPS)
    h = h.transpose(0, 2, 1, 3).reshape(B, S, M_WIDTH).astype(mq.dtype) * m_norm
    return h * jax.nn.sigmoid(mo)


def window_attn_branch(aq, ak, av, q_norm, k_norm, sink):
    B, S, _ = aq.shape
    NB = S // A_BLOCK
    q = rms_norm(aq.reshape(B, S, A_HEADS, A_HEAD_DIM), q_norm)
    k = rms_norm(ak.reshape(B, S, A_KV_HEADS, A_HEAD_DIM), k_norm)
    v = av.reshape(B, S, A_KV_HEADS, A_HEAD_DIM)
    pos = jnp.arange(S)
    q = rope(q, pos)
    k = rope(k, pos)

    def band(t):
        tp = jnp.pad(t, ((0, 0), (WINDOW, WINDOW), (0, 0), (0, 0)))
        tp = tp.reshape(B, NB + 2, A_BLOCK, A_KV_HEADS, A_HEAD_DIM)
        return jnp.concatenate([tp[:, :-2], tp[:, 1:-1], tp[:, 2:]], axis=2)

    kw = band(k)
    vw = band(v)
    qb = q.reshape(B, NB, A_BLOCK, A_KV_HEADS, A_GROUP, A_HEAD_DIM)
    s = jnp.einsum('bnqhgd,bnkhd->bnhgqk', qb, kw).astype(jnp.float32) * (A_HEAD_DIM ** -0.5)
    qi = jnp.arange(A_BLOCK)
    kj = jnp.arange(3 * A_BLOCK)
    rel = kj[None, :] - WINDOW - qi[:, None]
    kpos = jnp.arange(NB)[:, None] * A_BLOCK - WINDOW + kj[None, :]
    mask = (jnp.abs(rel) <= WINDOW)[None, :, :] & ((kpos >= 0) & (kpos < S))[:, None, :]
    s = jnp.where(mask[None, :, None, None, :, :], s, NEG)
    sk = sink.astype(jnp.float32).reshape(A_KV_HEADS, A_GROUP)[None, None, :, :, None, None]
    m = jnp.maximum(jnp.max(s, axis=-1, keepdims=True), sk)
    p = jnp.exp(s - m)
    den = jnp.sum(p, axis=-1, keepdims=True) + jnp.exp(sk - m)
    o = jnp.einsum('bnhgqk,bnkhd->bnqhgd', (p / den).astype(v.dtype), vw)
    return o.reshape(B, S, A_WIDTH)


def encoder_layer(x, ffn1_norm, ffn1_w1, ffn1_w3, ffn1_w2, mix_norm, w_in, b_gates, m_norm,
                  q_norm, k_norm, sink, w_pm, w_pa, w_out, ffn2_norm, ffn2_w1, ffn2_w3, ffn2_w2):
    x = x + 0.5 * swiglu(rms_norm(x, ffn1_norm), ffn1_w1, ffn1_w3, ffn1_w2)
    h = rms_norm(x, mix_norm)
    z = h @ w_in
    sizes = (M_WIDTH, M_WIDTH, M_WIDTH, M_WIDTH, N_GATE, A_WIDTH, A_KV_WIDTH, A_KV_WIDTH, D_MODEL, D_MODEL)
    idx = [sum(sizes[:i + 1]) for i in range(len(sizes) - 1)]
    mq, mk, mv, mo, mg, aq, ak, av, gm, ga = jnp.split(z, idx, axis=-1)
    hm = mlstm_branch(mq, mk, mv, mo, mg, b_gates, m_norm)
    ha = window_attn_branch(aq, ak, av, q_norm, k_norm, sink)
    merged = jax.nn.sigmoid(gm) * (hm @ w_pm) + jax.nn.sigmoid(ga) * (ha @ w_pa)
    x = x + merged @ w_out
    x = x + 0.5 * swiglu(rms_norm(x, ffn2_norm), ffn2_w1, ffn2_w3, ffn2_w2)
    return x


def setup_inputs(seed: int = 0) -> dict:
    key = jax.random.key(seed)
    ks = jax.random.split(key, 24)
    f32 = jnp.float32

    def nrm(k, shape, fan_in):
        return jax.random.normal(k, shape, f32) * (fan_in ** -0.5)

    def gain(k, shape):
        return 1.0 + 0.05 * jax.random.normal(k, shape, f32)

    gate_base = jnp.tile(jnp.concatenate([jnp.zeros((M_HEADS,), f32),
                                          jnp.linspace(3.0, 6.0, M_HEADS, dtype=f32)]), 2)
    return {
        'x_prompt': jax.random.normal(ks[0], (BATCH, SEQ, D_MODEL), f32),
        'x_sample': jax.random.normal(ks[1], (DEC_BATCH, DEC_SEQ, D_MODEL), f32),
        'ffn1_norm': gain(ks[2], (DEPTH, D_MODEL)),
        'ffn1_w1': nrm(ks[3], (DEPTH, D_MODEL, D_FF), D_MODEL),
        'ffn1_w3': nrm(ks[4], (DEPTH, D_MODEL, D_FF), D_MODEL),
        'ffn1_w2': nrm(ks[5], (DEPTH, D_FF, D_MODEL), D_FF),
        'mix_norm': gain(ks[6], (DEPTH, D_MODEL)),
        'w_in': nrm(ks[7], (DEPTH, D_MODEL, IN_COLS), D_MODEL),
        'b_gates': gate_base[None, :] + 0.1 * jax.random.normal(ks[8], (DEPTH, N_GATE), f32),
        'm_norm': gain(ks[9], (DEPTH, M_WIDTH)),
        'q_norm': gain(ks[10], (DEPTH, A_HEAD_DIM)),
        'k_norm': gain(ks[11], (DEPTH, A_HEAD_DIM)),
        'sink': 0.5 * jax.random.normal(ks[12], (DEPTH, A_HEADS), f32),
        'w_pm': nrm(ks[13], (DEPTH, M_WIDTH, D_MODEL), M_WIDTH),
        'w_pa': nrm(ks[14], (DEPTH, A_WIDTH, D_MODEL), A_WIDTH),
        'w_out': nrm(ks[15], (DEPTH, D_MODEL, D_MODEL), D_MODEL),
        'ffn2_norm': gain(ks[16], (DEPTH, D_MODEL)),
        'ffn2_w1': nrm(ks[17], (DEPTH, D_MODEL, D_FF), D_MODEL),
        'ffn2_w3': nrm(ks[18], (DEPTH, D_MODEL, D_FF), D_MODEL),
        'ffn2_w2': nrm(ks[19], (DEPTH, D_FF, D_MODEL), D_FF),
    }


def reference(x_prompt, x_sample, ffn1_norm, ffn1_w1, ffn1_w3, ffn1_w2, mix_norm, w_in, b_gates,
              m_norm, q_norm, k_norm, sink, w_pm, w_pa, w_out, ffn2_norm, ffn2_w1, ffn2_w3, ffn2_w2):
    y_prompt = x_prompt
    y_sample = x_sample
    for l in range(DEPTH):
        y_prompt = encoder_layer(y_prompt, ffn1_norm[l], ffn1_w1[l], ffn1_w3[l], ffn1_w2[l], mix_norm[l],
                                 w_in[l], b_gates[l], m_norm[l], q_norm[l], k_norm[l], sink[l], w_pm[l],
                                 w_pa[l], w_out[l], ffn2_norm[l], ffn2_w1[l], ffn2_w3[l], ffn2_w2[l])
        y_sample = encoder_layer(y_sample, ffn1_norm[l], ffn1_w1[l], ffn1_w3[l], ffn1_w2[l], mix_norm[l],
                                 w_in[l], b_gates[l], m_norm[l], q_norm[l], k_norm[l], sink[l], w_pm[l],
                                 w_pa[l], w_out[l], ffn2_norm[l], ffn2_w1[l], ffn2_w3[l], ffn2_w2[l])
    return (y_prompt, y_sample)
```

```python
import functools

import numpy as np
import jax
import jax.numpy as jnp
from jax import lax
from jax.experimental import pallas as pl
from jax.experimental.pallas import tpu as pltpu

F32 = jnp.float32
BF16 = jnp.bfloat16

D_MODEL = 1024
D_FF = 2816
M_HEADS = 4
M_HEAD_DIM = 128
M_WIDTH = 512
N_GATE = 16
A_HEADS = 8
A_KV_HEADS = 2
A_GROUP = 4
A_HEAD_DIM = 64
A_WIDTH = 512
WINDOW = 128
ROPE_THETA = 10000.0
EPS = 1e-6
NEG = -1e30

LANES = 128
FF_CHUNK = 256
N_FF_CHUNKS = D_FF // FF_CHUNK
Z_CHUNK = 512
Z_COLS = 5120
N_Z_CHUNKS = Z_COLS // Z_CHUNK
N_Z_BLOCKS = Z_COLS // LANES
ZB_MQ, ZB_MK, ZB_MV, ZB_MO = 0, 4, 8, 12
ZB_AQ, ZB_AK, ZB_AV = 16, 20, 22
ZB_GM, ZB_GA = 24, 32
M_CHUNK = 128
TOKEN_TILE = 512
VMEM_LIMIT_BYTES = 60 * 1024 * 1024


def _rms(x, g):
    return x * lax.rsqrt(jnp.mean(x * x, axis=-1, keepdims=True) + EPS) * g


def _const_spec(shape):
    nd = len(shape)
    return pl.BlockSpec(shape, lambda *_: (0,) * nd, pipeline_mode=pl.Buffered(1))


def _swiglu_into(h_ref, w1_ref, w3_ref, w2_ref, acc_ref):
    acc_ref[...] = jnp.zeros_like(acc_ref)

    def body(c, carry):
        h = h_ref[...]
        a = jnp.dot(h, w1_ref[c], preferred_element_type=F32)
        b = jnp.dot(h, w3_ref[c], preferred_element_type=F32)
        hid = (a * jax.nn.sigmoid(a) * b).astype(BF16)
        acc_ref[...] += jnp.dot(hid, w2_ref[c], preferred_element_type=F32)
        return carry

    lax.fori_loop(0, N_FF_CHUNKS, body, 0)


def _front_kernel(x_ref, n1_ref, w1_ref, w3_ref, w2_ref, nm_ref, wz_ref, wg_ref,
                  x1_ref, z_ref, g_ref, h_ref, acc_ref):
    x = x_ref[...]
    h_ref[...] = _rms(x, n1_ref[...]).astype(BF16)
    _swiglu_into(h_ref, w1_ref, w3_ref, w2_ref, acc_ref)
    x1 = x + 0.5 * acc_ref[...]
    x1_ref[...] = x1
    h_ref[...] = _rms(x1, nm_ref[...]).astype(BF16)

    def proj(j, carry):
        z = jnp.dot(h_ref[...], wz_ref[j], preferred_element_type=F32)
        for q in range(Z_CHUNK // LANES):
            z_ref[j * (Z_CHUNK // LANES) + q] = z[:, q * LANES:(q + 1) * LANES].astype(BF16)
        return carry

    lax.fori_loop(0, N_Z_CHUNKS, proj, 0)
    g_ref[...] = jnp.dot(h_ref[...], wg_ref[...], preferred_element_type=F32)


def _front_call(x, n1, w1, w3, w2, nm, wz, wg):
    n = x.shape[0]
    tm = TOKEN_TILE
    return pl.pallas_call(
        _front_kernel,
        grid=(n // tm,),
        in_specs=[
            pl.BlockSpec((tm, D_MODEL), lambda i: (i, 0)),
            _const_spec((1, D_MODEL)),
            _const_spec((N_FF_CHUNKS, D_MODEL, FF_CHUNK)),
            _const_spec((N_FF_CHUNKS, D_MODEL, FF_CHUNK)),
            _const_spec((N_FF_CHUNKS, FF_CHUNK, D_MODEL)),
            _const_spec((1, D_MODEL)),
            _const_spec((N_Z_CHUNKS, D_MODEL, Z_CHUNK)),
            _const_spec((D_MODEL, LANES)),
        ],
        out_specs=[
            pl.BlockSpec((tm, D_MODEL), lambda i: (i, 0)),
            pl.BlockSpec((N_Z_BLOCKS, tm, LANES), lambda i: (0, i, 0)),
            pl.BlockSpec((tm, LANES), lambda i: (i, 0)),
        ],
        out_shape=[
            jax.ShapeDtypeStruct((n, D_MODEL), F32),
            jax.ShapeDtypeStruct((N_Z_BLOCKS, n, LANES), BF16),
            jax.ShapeDtypeStruct((n, LANES), F32),
        ],
        scratch_shapes=[
            pltpu.VMEM((tm, D_MODEL), BF16),
            pltpu.VMEM((tm, D_MODEL), F32),
        ],
        compiler_params=pltpu.CompilerParams(
            dimension_semantics=("arbitrary",), vmem_limit_bytes=VMEM_LIMIT_BYTES),
        name="front_ffn_proj",
    )(x, n1, w1, w3, w2, nm, wz, wg)


def _back_kernel(x1_ref, gm_ref, ga_ref, hm_ref, ha_ref, wpm_ref, wpa_ref, wo_ref,
                 n2_ref, w1_ref, w3_ref, w2_ref, y_ref, h_ref, acc_ref):
    pm = jnp.dot(hm_ref[...], wpm_ref[...], preferred_element_type=F32)
    pa = jnp.dot(ha_ref[...], wpa_ref[...], preferred_element_type=F32)
    for j in range(D_MODEL // LANES):
        sl = slice(j * LANES, (j + 1) * LANES)
        gm = jax.nn.sigmoid(gm_ref[j].astype(F32))
        ga = jax.nn.sigmoid(ga_ref[j].astype(F32))
        h_ref[:, sl] = (gm * pm[:, sl] + ga * pa[:, sl]).astype(BF16)
    x2 = x1_ref[...] + jnp.dot(h_ref[...], wo_ref[...], preferred_element_type=F32)
    h_ref[...] = _rms(x2, n2_ref[...]).astype(BF16)
    _swiglu_into(h_ref, w1_ref, w3_ref, w2_ref, acc_ref)
    y_ref[...] = x2 + 0.5 * acc_ref[...]


def _back_call(x1, z, hm, ha, wpm, wpa, wo, n2, w1, w3, w2):
    n = x1.shape[0]
    tm = TOKEN_TILE
    gblk = D_MODEL // LANES
    return pl.pallas_call(
        _back_kernel,
        grid=(n // tm,),
        in_specs=[
            pl.BlockSpec((tm, D_MODEL), lambda i: (i, 0)),
            pl.BlockSpec((gblk, tm, LANES), lambda i: (ZB_GM // gblk, i, 0)),
            pl.BlockSpec((gblk, tm, LANES), lambda i: (ZB_GA // gblk, i, 0)),
            pl.BlockSpec((tm, M_WIDTH), lambda i: (i, 0)),
            pl.BlockSpec((tm, A_WIDTH), lambda i: (i, 0)),
            _const_spec((M_WIDTH, D_MODEL)),
            _const_spec((A_WIDTH, D_MODEL)),
            _const_spec((D_MODEL, D_MODEL)),
            _const_spec((1, D_MODEL)),
            _const_spec((N_FF_CHUNKS, D_MODEL, FF_CHUNK)),
            _const_spec((N_FF_CHUNKS, D_MODEL, FF_CHUNK)),
            _const_spec((N_FF_CHUNKS, FF_CHUNK, D_MODEL)),
        ],
        out_specs=pl.BlockSpec((tm, D_MODEL), lambda i: (i, 0)),
        out_shape=jax.ShapeDtypeStruct((n, D_MODEL), F32),
        scratch_shapes=[
            pltpu.VMEM((tm, D_MODEL), BF16),
            pltpu.VMEM((tm, D_MODEL), F32),
        ],
        compiler_params=pltpu.CompilerParams(
            dimension_semantics=("arbitrary",), vmem_limit_bytes=VMEM_LIMIT_BYTES),
        name="back_merge_ffn",
    )(x1, z, z, hm, ha, wpm, wpa, wo, n2, w1, w3, w2)


def _split3(x):
    h1 = x.astype(BF16)
    r1 = x - h1.astype(F32)
    h2 = r1.astype(BF16)
    h3 = (r1 - h2.astype(F32)).astype(BF16)
    return h1, h2, h3


def _mlstm_kernel(q_ref, k_ref, v_ref, o_ref, g_ref, bg_ref, mn_ref, out_ref,
                  gi_ref, lf_ref, pf_ref, sf_ref, kt_ref, va_ref,
                  cf_ref, cb_ref, mf_ref, mb_ref, st_ref, *, seq):
    L = M_CHUNK
    nc = seq // L
    hd = pl.program_id(1)
    row = lax.broadcasted_iota(jnp.int32, (L, L), 0)
    col = lax.broadcasted_iota(jnp.int32, (L, L), 1)
    lower = col <= row
    upper = col >= row

    @pl.when(hd == 0)
    def _():
        def tr(c, carry):
            gc = g_ref[pl.ds(pl.multiple_of(c * L, L), L), :] + bg_ref[...]
            gt = gc.T[0:N_GATE, :]
            rows = pl.ds(pl.multiple_of(c * N_GATE, N_GATE), N_GATE)
            gi_ref[rows, :] = gt
            lf_ref[rows, :] = jnp.minimum(gt, 0.0) - jnp.log1p(jnp.exp(-jnp.abs(gt)))
            return carry

        lax.fori_loop(0, nc, tr, 0)
        h1, h2, h3 = _split3(lf_ref[...])
        incl_prefix = upper.astype(BF16)
        incl_suffix = lower.astype(BF16)
        pf_ref[...] = (jnp.dot(h1, incl_prefix, preferred_element_type=F32)
                       + jnp.dot(h2, incl_prefix, preferred_element_type=F32)
                       + jnp.dot(h3, incl_prefix, preferred_element_type=F32))
        sf_ref[...] = (jnp.dot(h1, incl_suffix, preferred_element_type=F32)
                       + jnp.dot(h2, incl_suffix, preferred_element_type=F32)
                       + jnp.dot(h3, incl_suffix, preferred_element_type=F32))

    def trk(c, carry):
        cs = pl.ds(pl.multiple_of(c * L, L), L)
        kc = k_ref[0, cs, :].astype(F32) * (M_HEAD_DIM ** -0.5)
        kt_ref[:, cs] = kc.T
        return carry

    lax.fori_loop(0, nc, trk, 0)
    va_ref[:, 0:M_HEAD_DIM] = v_ref[0]
    va_ref[:, M_HEAD_DIM:2 * M_HEAD_DIM] = jnp.ones((seq, M_HEAD_DIM), BF16)

    def scan(c_ref, m_ref, cum_ref, i_off, f_off, g_lane, order):
        st_ref[...] = jnp.zeros_like(st_ref)

        def step(i, m):
            c = order(i)
            cs = pl.ds(pl.multiple_of(c * L, L), L)
            li = gi_ref[pl.ds(c * N_GATE + i_off + hd, 1), :]
            bc = cum_ref[pl.ds(c * N_GATE + f_off + hd, 1), :]
            g = bc[:, g_lane:g_lane + 1]
            a = g + (li - bc)
            m_loc = jnp.max(a, axis=1, keepdims=True)
            w = jnp.exp(a - m_loc)
            kw = (kt_ref[:, cs] * w).astype(BF16)
            dcn = jnp.dot(kw, va_ref[cs, :], preferred_element_type=F32)
            st = st_ref[...]
            c_ref[c] = st.astype(BF16)
            m_ref[c] = jnp.broadcast_to(m, (8, LANES))
            m_new = jnp.maximum(g + m, m_loc)
            st_ref[...] = jnp.exp(g + m - m_new) * st + jnp.exp(m_loc - m_new) * dcn
            return m_new

        lax.fori_loop(0, nc, step, jnp.full((1, 1), NEG, F32))

    scan(cf_ref, mf_ref, pf_ref, 0, 4, L - 1, lambda i: i)
    scan(cb_ref, mb_ref, sf_ref, 8, 12, 0, lambda i: nc - 1 - i)

    def out_step(c, carry):
        cs = pl.ds(pl.multiple_of(c * L, L), L)
        qc = q_ref[0, cs, :]
        qf = qc.astype(F32)
        sm = jnp.dot(qc, kt_ref[:, cs].astype(BF16), preferred_element_type=F32)
        vac = va_ref[cs, :]
        hsum = jnp.zeros((L, M_HEAD_DIM), F32)
        for (mask, c_ref, m_ref, cum_ref, i_off, f_off) in (
                (lower, cf_ref, mf_ref, pf_ref, 0, 4),
                (upper, cb_ref, mb_ref, sf_ref, 8, 12)):
            li = gi_ref[pl.ds(c * N_GATE + i_off + hd, 1), :]
            bc = cum_ref[pl.ds(c * N_GATE + f_off + hd, 1), :]
            lf = lf_ref[pl.ds(c * N_GATE + f_off + hd, 1), :]
            mp = m_ref[c][0:1, :]
            rm = jnp.where(mask, li - bc, NEG)
            cm = jnp.maximum(jnp.max(rm, axis=1, keepdims=True), mp)
            p = jnp.exp(rm - cm)
            wq = (p * sm).astype(BF16)
            sq = (jnp.exp(mp - cm) * qf).astype(BF16)
            y = (jnp.dot(wq, vac, preferred_element_type=F32)
                 + jnp.dot(sq, c_ref[c], preferred_element_type=F32))
            bcc = jnp.sum(jnp.where(mask, lf, 0.0), axis=1, keepdims=True)
            floor = jnp.exp(-(bcc + cm))
            hsum = hsum + y[:, 0:M_HEAD_DIM] / jnp.maximum(
                jnp.abs(y[:, M_HEAD_DIM:2 * M_HEAD_DIM]), floor)
        hn = hsum * lax.rsqrt(jnp.mean(hsum * hsum, axis=-1, keepdims=True) + EPS)
        gate = jax.nn.sigmoid(o_ref[0, cs, :].astype(F32))
        out_ref[cs, :] = (hn * mn_ref[...] * gate).astype(BF16)
        return carry

    lax.fori_loop(0, nc, out_step, 0)


def _mlstm_call(z, g, bg, mn, batch, seq):
    n = batch * seq
    nc = seq // M_CHUNK

    def zspec(base):
        return pl.BlockSpec((1, seq, LANES), lambda b, h: (base + h, b, 0))

    return pl.pallas_call(
        functools.partial(_mlstm_kernel, seq=seq),
        grid=(batch, M_HEADS),
        in_specs=[
            zspec(ZB_MQ), zspec(ZB_MK), zspec(ZB_MV), zspec(ZB_MO),
            pl.BlockSpec((seq, LANES), lambda b, h: (b, 0)),
            pl.BlockSpec((1, LANES), lambda b, h: (0, 0)),
            pl.BlockSpec((1, LANES), lambda b, h: (0, h)),
        ],
        out_specs=pl.BlockSpec((seq, LANES), lambda b, h: (b, h)),
        out_shape=jax.ShapeDtypeStruct((n, M_WIDTH), BF16),
        scratch_shapes=[
            pltpu.VMEM((nc * N_GATE, LANES), F32),
            pltpu.VMEM((nc * N_GATE, LANES), F32),
            pltpu.VMEM((nc * N_GATE, LANES), F32),
            pltpu.VMEM((nc * N_GATE, LANES), F32),
            pltpu.VMEM((M_HEAD_DIM, seq), F32),
            pltpu.VMEM((seq, 2 * M_HEAD_DIM), BF16),
            pltpu.VMEM((nc, M_HEAD_DIM, 2 * M_HEAD_DIM), BF16),
            pltpu.VMEM((nc, M_HEAD_DIM, 2 * M_HEAD_DIM), BF16),
            pltpu.VMEM((nc, 8, LANES), F32),
            pltpu.VMEM((nc, 8, LANES), F32),
            pltpu.VMEM((M_HEAD_DIM, 2 * M_HEAD_DIM), F32),
        ],
        compiler_params=pltpu.CompilerParams(
            dimension_semantics=("arbitrary", "arbitrary"), vmem_limit_bytes=VMEM_LIMIT_BYTES),
        name="mlstm",
    )(z, z, z, z, g, bg, mn)


def _attn_kernel(sink_ref, q_ref, k_ref, v_ref, cos_ref, sin_ref, qn_ref, kn_ref, out_ref,
                 qs_ref, ks_ref, vs_ref, *, seq):
    W = WINDOW
    nb = seq // W
    kv = pl.program_id(1)
    lane = lax.broadcasted_iota(jnp.int32, (1, LANES), 1)
    first_half = (lane % A_HEAD_DIM) < (A_HEAD_DIM // 2)
    head_lo = lane < A_HEAD_DIM
    gr = lax.broadcasted_iota(jnp.int32, (LANES, LANES), 0) // A_HEAD_DIM
    gc = lax.broadcasted_iota(jnp.int32, (LANES, LANES), 1) // A_HEAD_DIM
    group_mean = jnp.where(gr == gc, 1.0 / A_HEAD_DIM, 0.0).astype(BF16)
    rows_blk = 256

    def norm_rope(x, gain, cs, sn):
        sq = x * x
        hi = sq.astype(BF16)
        lo = (sq - hi.astype(F32)).astype(BF16)
        ms = (jnp.dot(hi, group_mean, preferred_element_type=F32)
              + jnp.dot(lo, group_mean, preferred_element_type=F32))
        xn = x * lax.rsqrt(ms + EPS) * gain
        rot = jnp.where(first_half,
                        -pltpu.roll(xn, LANES - A_HEAD_DIM // 2, 1),
                        pltpu.roll(xn, A_HEAD_DIM // 2, 1))
        return xn * cs + rot * sn

    def prep(i, carry):
        rs = pl.ds(pl.multiple_of(i * rows_blk, rows_blk), rows_blk)
        cs = cos_ref[rs, :]
        sn = sin_ref[rs, :]
        for j in range(2):
            qv = norm_rope(q_ref[j, rs, :].astype(F32), qn_ref[...], cs, sn)
            qs_ref[j, rs, :] = (qv * (A_HEAD_DIM ** -0.5)).astype(BF16)
        kvv = norm_rope(k_ref[0, rs, :].astype(F32), kn_ref[...], cs, sn)
        ks_ref[pl.ds(pl.multiple_of(W + i * rows_blk, W), rows_blk), :] = kvv.astype(BF16)
        return carry

    lax.fori_loop(0, seq // rows_blk, prep, 0)
    zpad = jnp.zeros((W, LANES), BF16)
    ks_ref[0:W, :] = zpad
    ks_ref[seq + W:seq + 2 * W, :] = zpad
    vs_ref[0:W, :] = zpad
    vs_ref[seq + W:seq + 2 * W, :] = zpad
    vs_ref[W:seq + W, :] = v_ref[0]

    qi = lax.broadcasted_iota(jnp.int32, (W, W), 0)
    kj = lax.broadcasted_iota(jnp.int32, (W, W), 1)
    left_band = jnp.where(kj >= qi, 0.0, NEG)
    right_band = jnp.where(kj <= qi, 0.0, NEG)
    zeros_mid = jnp.zeros((W, W), F32)

    def blk(n, carry):
        ws = pl.ds(pl.multiple_of(n * W, W), 3 * W)
        qrows = pl.ds(pl.multiple_of(n * W, W), W)
        kwin = ks_ref[ws, :]
        vwin = vs_ref[ws, :]
        q0 = qs_ref[0, qrows, :]
        q1 = qs_ref[1, qrows, :]
        zq = jnp.zeros_like(q0)
        qstack = jnp.concatenate([
            jnp.where(head_lo, q0, zq), jnp.where(head_lo, zq, q0),
            jnp.where(head_lo, q1, zq), jnp.where(head_lo, zq, q1)], axis=0)
        s = lax.dot_general(qstack, kwin, (((1,), (1,)), ((), ())),
                            preferred_element_type=F32)
        edge_l = jnp.where(n == 0, NEG, 0.0)
        edge_r = jnp.where(n == nb - 1, NEG, 0.0)
        bias = jnp.concatenate([left_band + edge_l, zeros_mid, right_band + edge_r], axis=1)
        outs = []
        for hh in range(A_GROUP):
            sh = s[hh * W:(hh + 1) * W, :] + bias
            sk = sink_ref[kv * A_GROUP + hh]
            m = jnp.maximum(jnp.max(sh, axis=1, keepdims=True), sk)
            p = jnp.exp(sh - m)
            den = jnp.sum(p, axis=1, keepdims=True) + jnp.exp(sk - m)
            pv = jnp.dot(p.astype(BF16), vwin, preferred_element_type=F32)
            outs.append(pv / den)
        out_ref[qrows, 0:LANES] = jnp.where(head_lo, outs[0], outs[1]).astype(BF16)
        out_ref[qrows, LANES:2 * LANES] = jnp.where(head_lo, outs[2], outs[3]).astype(BF16)
        return carry

    lax.fori_loop(0, nb, blk, 0)


def _attn_call(z, sink, cos_t, sin_t, qn, kn, batch, seq):
    n = batch * seq
    return pl.pallas_call(
        functools.partial(_attn_kernel, seq=seq),
        grid=(batch, A_KV_HEADS),
        in_specs=[
            pl.BlockSpec(memory_space=pltpu.SMEM),
            pl.BlockSpec((2, seq, LANES), lambda b, g: (ZB_AQ // 2 + g, b, 0)),
            pl.BlockSpec((1, seq, LANES), lambda b, g: (ZB_AK + g, b, 0)),
            pl.BlockSpec((1, seq, LANES), lambda b, g: (ZB_AV + g, b, 0)),
            pl.BlockSpec((seq, LANES), lambda b, g: (0, 0)),
            pl.BlockSpec((seq, LANES), lambda b, g: (0, 0)),
            pl.BlockSpec((1, LANES), lambda b, g: (0, 0)),
            pl.BlockSpec((1, LANES), lambda b, g: (0, 0)),
        ],
        out_specs=pl.BlockSpec((seq, 2 * LANES), lambda b, g: (b, g)),
        out_shape=jax.ShapeDtypeStruct((n, A_WIDTH), BF16),
        scratch_shapes=[
            pltpu.VMEM((2, seq, LANES), BF16),
            pltpu.VMEM((seq + 2 * WINDOW, LANES), BF16),
            pltpu.VMEM((seq + 2 * WINDOW, LANES), BF16),
        ],
        compiler_params=pltpu.CompilerParams(
            dimension_semantics=("arbitrary", "arbitrary"), vmem_limit_bytes=VMEM_LIMIT_BYTES),
        name="window_attn",
    )(sink, z, z, z, cos_t, sin_t, qn, kn)


def _z_column_index():
    ak, av = 2576, 2704
    parts = [np.arange(0, 2048), np.arange(2064, 2576)]
    for base in (ak, av):
        for g in range(A_KV_HEADS):
            cols = np.arange(base + g * A_HEAD_DIM, base + (g + 1) * A_HEAD_DIM)
            parts += [cols, cols]
    parts.append(np.arange(2832, 4880))
    return np.concatenate(parts)


def _ffn_weights(w1, w3, w2):
    def cols(w):
        return w.astype(BF16).reshape(D_MODEL, N_FF_CHUNKS, FF_CHUNK).transpose(1, 0, 2)
    return cols(w1), cols(w3), w2.astype(BF16).reshape(N_FF_CHUNKS, FF_CHUNK, D_MODEL)


def _rope_tables(seq):
    half = A_HEAD_DIM // 2
    inv = jnp.power(ROPE_THETA, -jnp.arange(half, dtype=F32) / half)
    ang = jnp.arange(seq).astype(F32)[:, None] * inv[None, :]
    return jnp.tile(jnp.cos(ang), (1, LANES // half)), jnp.tile(jnp.sin(ang), (1, LANES // half))


def _layer(x, p):
    batch, seq, _ = x.shape
    xf = x.reshape(batch * seq, D_MODEL)
    x1, z, g = _front_call(xf, p["n1"], *p["ffn1"], p["nm"], p["wz"], p["wg"])
    hm = _mlstm_call(z, g, p["bg"], p["mn"], batch, seq)
    cos_t, sin_t = _rope_tables(seq)
    ha = _attn_call(z, p["sink"], cos_t, sin_t, p["qn"], p["kn"], batch, seq)
    y = _back_call(x1, z, hm, ha, p["wpm"], p["wpa"], p["wo"], p["n2"], *p["ffn2"])
    return y.reshape(batch, seq, D_MODEL)


def kernel(x_prompt, x_sample, ffn1_norm, ffn1_w1, ffn1_w3, ffn1_w2, mix_norm, w_in, b_gates, m_norm, q_norm, k_norm, sink, w_pm, w_pa, w_out, ffn2_norm, ffn2_w1, ffn2_w3, ffn2_w2):
    depth = w_in.shape[0]
    zidx = _z_column_index()
    y_prompt, y_sample = x_prompt, x_sample
    for l in range(depth):
        wz = w_in[l][:, zidx].astype(BF16)
        p = {
            "n1": ffn1_norm[l][None, :],
            "ffn1": _ffn_weights(ffn1_w1[l], ffn1_w3[l], ffn1_w2[l]),
            "nm": mix_norm[l][None, :],
            "wz": wz.reshape(D_MODEL, N_Z_CHUNKS, Z_CHUNK).transpose(1, 0, 2),
            "wg": jnp.pad(w_in[l][:, 2048:2048 + N_GATE], ((0, 0), (0, LANES - N_GATE))).astype(BF16),
            "bg": jnp.pad(b_gates[l], (0, LANES - N_GATE))[None, :],
            "mn": m_norm[l][None, :],
            "qn": jnp.tile(q_norm[l], LANES // A_HEAD_DIM)[None, :],
            "kn": jnp.tile(k_norm[l], LANES // A_HEAD_DIM)[None, :],
            "sink": sink[l],
            "wpm": w_pm[l].astype(BF16),
            "wpa": w_pa[l].astype(BF16),
            "wo": w_out[l].astype(BF16),
            "n2": ffn2_norm[l][None, :],
            "ffn2": _ffn_weights(ffn2_w1[l], ffn2_w3[l], ffn2_w2[l]),
        }
        y_prompt = _layer(y_prompt, p)
        y_sample = _layer(y_sample, p)
    return (y_prompt, y_sample)
```

```python
import functools

import numpy as np
import jax
import jax.numpy as jnp
from jax import lax
from jax.experimental import pallas as pl
from jax.experimental.pallas import tpu as pltpu

F32 = jnp.float32
BF16 = jnp.bfloat16

D_MODEL = 1024
D_FF = 2816
M_HEADS = 4
M_HEAD_DIM = 128
M_WIDTH = 512
N_GATE = 16
A_HEADS = 8
A_KV_HEADS = 2
A_GROUP = 4
A_HEAD_DIM = 64
A_WIDTH = 512
WINDOW = 128
ROPE_THETA = 10000.0
EPS = 1e-6
NEG = -1e30
LOG2E = 1.4426950408889634

LANES = 128
FF_CHUNK = 256
N_FF_CHUNKS = D_FF // FF_CHUNK
Z_CHUNK = 512
Z_COLS = 5120
N_Z_CHUNKS = Z_COLS // Z_CHUNK
N_Z_BLOCKS = Z_COLS // LANES
ZB_MQ, ZB_MK, ZB_MV, ZB_MO = 0, 4, 8, 12
ZB_AQ, ZB_AK, ZB_AV = 16, 20, 22
ZB_GM, ZB_GA = 24, 32
M_CHUNK = 128
ONES_ROWS = 16
VA_ROWS = M_HEAD_DIM + ONES_ROWS
TOKEN_TILE = 512
VMEM_LIMIT_BYTES = 60 * 1024 * 1024


def _rms(x, g):
    return x * lax.rsqrt(jnp.mean(x * x, axis=-1, keepdims=True) + EPS) * g


def _const_spec(shape):
    nd = len(shape)
    return pl.BlockSpec(shape, lambda *_: (0,) * nd, pipeline_mode=pl.Buffered(1))


def _swiglu_into(h_ref, w1_ref, w3_ref, w2_ref, acc_ref):
    for c in range(N_FF_CHUNKS):
        h = h_ref[...]
        a = jnp.dot(h, w1_ref[c], preferred_element_type=F32)
        b = jnp.dot(h, w3_ref[c], preferred_element_type=F32)
        hid = (a * jax.nn.sigmoid(a) * b).astype(BF16)
        part = jnp.dot(hid, w2_ref[c], preferred_element_type=F32)
        if c == 0:
            acc_ref[...] = part
        else:
            acc_ref[...] += part


def _front_kernel(x_ref, n1_ref, w1_ref, w3_ref, w2_ref, nm_ref, wz_ref, wg_ref,
                  x1_ref, z_ref, g_ref, h_ref, acc_ref):
    x = x_ref[...]
    h_ref[...] = _rms(x, n1_ref[...]).astype(BF16)
    _swiglu_into(h_ref, w1_ref, w3_ref, w2_ref, acc_ref)
    x1 = x + 0.5 * acc_ref[...]
    x1_ref[...] = x1
    h_ref[...] = _rms(x1, nm_ref[...]).astype(BF16)

    for j in range(N_Z_CHUNKS):
        z = jnp.dot(h_ref[...], wz_ref[j], preferred_element_type=F32)
        for q in range(Z_CHUNK // LANES):
            z_ref[j * (Z_CHUNK // LANES) + q] = z[:, q * LANES:(q + 1) * LANES].astype(BF16)
    g_ref[...] = jnp.dot(h_ref[...], wg_ref[...], preferred_element_type=F32)


def _front_call(x, n1, w1, w3, w2, nm, wz, wg):
    n = x.shape[0]
    tm = TOKEN_TILE
    return pl.pallas_call(
        _front_kernel,
        grid=(n // tm,),
        in_specs=[
            pl.BlockSpec((tm, D_MODEL), lambda i: (i, 0)),
            _const_spec((1, D_MODEL)),
            _const_spec((N_FF_CHUNKS, D_MODEL, FF_CHUNK)),
            _const_spec((N_FF_CHUNKS, D_MODEL, FF_CHUNK)),
            _const_spec((N_FF_CHUNKS, FF_CHUNK, D_MODEL)),
            _const_spec((1, D_MODEL)),
            _const_spec((N_Z_CHUNKS, D_MODEL, Z_CHUNK)),
            _const_spec((D_MODEL, LANES)),
        ],
        out_specs=[
            pl.BlockSpec((tm, D_MODEL), lambda i: (i, 0)),
            pl.BlockSpec((N_Z_BLOCKS, tm, LANES), lambda i: (0, i, 0)),
            pl.BlockSpec((tm, LANES), lambda i: (i, 0)),
        ],
        out_shape=[
            jax.ShapeDtypeStruct((n, D_MODEL), F32),
            jax.ShapeDtypeStruct((N_Z_BLOCKS, n, LANES), BF16),
            jax.ShapeDtypeStruct((n, LANES), F32),
        ],
        scratch_shapes=[
            pltpu.VMEM((tm, D_MODEL), BF16),
            pltpu.VMEM((tm, D_MODEL), F32),
        ],
        compiler_params=pltpu.CompilerParams(
            dimension_semantics=("arbitrary",), vmem_limit_bytes=VMEM_LIMIT_BYTES),
        name="front_ffn_proj",
    )(x, n1, w1, w3, w2, nm, wz, wg)


def _back_kernel(x1_ref, gm_ref, ga_ref, hm_ref, ha_ref, wpm_ref, wpa_ref, wo_ref,
                 n2_ref, w1_ref, w3_ref, w2_ref, y_ref, h_ref, acc_ref):
    pm = jnp.dot(hm_ref[...], wpm_ref[...], preferred_element_type=F32)
    pa = jnp.dot(ha_ref[...], wpa_ref[...], preferred_element_type=F32)
    for j in range(D_MODEL // LANES):
        sl = slice(j * LANES, (j + 1) * LANES)
        gm = jax.nn.sigmoid(gm_ref[j].astype(F32))
        ga = jax.nn.sigmoid(ga_ref[j].astype(F32))
        h_ref[:, sl] = (gm * pm[:, sl] + ga * pa[:, sl]).astype(BF16)
    x2 = x1_ref[...] + jnp.dot(h_ref[...], wo_ref[...], preferred_element_type=F32)
    h_ref[...] = _rms(x2, n2_ref[...]).astype(BF16)
    _swiglu_into(h_ref, w1_ref, w3_ref, w2_ref, acc_ref)
    y_ref[...] = x2 + 0.5 * acc_ref[...]


def _back_call(x1, z, hm, ha, wpm, wpa, wo, n2, w1, w3, w2):
    n = x1.shape[0]
    tm = TOKEN_TILE
    gblk = D_MODEL // LANES
    return pl.pallas_call(
        _back_kernel,
        grid=(n // tm,),
        in_specs=[
            pl.BlockSpec((tm, D_MODEL), lambda i: (i, 0)),
            pl.BlockSpec((gblk, tm, LANES), lambda i: (ZB_GM // gblk, i, 0)),
            pl.BlockSpec((gblk, tm, LANES), lambda i: (ZB_GA // gblk, i, 0)),
            pl.BlockSpec((tm, M_WIDTH), lambda i: (i, 0)),
            pl.BlockSpec((tm, A_WIDTH), lambda i: (i, 0)),
            _const_spec((M_WIDTH, D_MODEL)),
            _const_spec((A_WIDTH, D_MODEL)),
            _const_spec((D_MODEL, D_MODEL)),
            _const_spec((1, D_MODEL)),
            _const_spec((N_FF_CHUNKS, D_MODEL, FF_CHUNK)),
            _const_spec((N_FF_CHUNKS, D_MODEL, FF_CHUNK)),
            _const_spec((N_FF_CHUNKS, FF_CHUNK, D_MODEL)),
        ],
        out_specs=pl.BlockSpec((tm, D_MODEL), lambda i: (i, 0)),
        out_shape=jax.ShapeDtypeStruct((n, D_MODEL), F32),
        scratch_shapes=[
            pltpu.VMEM((tm, D_MODEL), BF16),
            pltpu.VMEM((tm, D_MODEL), F32),
        ],
        compiler_params=pltpu.CompilerParams(
            dimension_semantics=("arbitrary",), vmem_limit_bytes=VMEM_LIMIT_BYTES),
        name="back_merge_ffn",
    )(x1, z, z, hm, ha, wpm, wpa, wo, n2, w1, w3, w2)


def _split3(x):
    h1 = x.astype(BF16)
    r1 = x - h1.astype(F32)
    h2 = r1.astype(BF16)
    h3 = (r1 - h2.astype(F32)).astype(BF16)
    return h1, h2, h3


def _mlstm_kernel(q_ref, k_ref, v_ref, o_ref, g_ref, bg_ref, mn_ref, out_ref,
                  gi_ref, lf_ref, pf_ref, sf_ref, qt_ref, va_ref, w_ref, mp_ref, mq_ref,
                  so_ref, sn_ref, dcn_ref, cf_ref, cb_ref, st_ref, *, seq):
    L = M_CHUNK
    nc = seq // L
    hd = pl.program_id(1)
    row = lax.broadcasted_iota(jnp.int32, (L, L), 0)
    col = lax.broadcasted_iota(jnp.int32, (L, L), 1)
    lower = col <= row
    upper = col >= row
    diag = col == row
    dirs = ((upper, cf_ref, pf_ref, 0, 4, L - 1, False),
            (lower, cb_ref, sf_ref, 8, 12, 0, True))

    @pl.when(hd == 0)
    def _():
        def tr(c, carry):
            gc = g_ref[pl.ds(pl.multiple_of(c * L, L), L), :] + bg_ref[...]
            gt = gc.T[0:N_GATE, :]
            ls = jnp.minimum(gt, 0.0) - jnp.log1p(jnp.exp(-jnp.abs(gt)))
            for j in range(N_GATE):
                gi_ref[pl.ds(j * nc + c, 1), :] = gt[j:j + 1, :] * LOG2E
                lf_ref[pl.ds(j * nc + c, 1), :] = ls[j:j + 1, :] * LOG2E
            return carry

        lax.fori_loop(0, nc, tr, 0)
        h1, h2, h3 = _split3(lf_ref[...])
        incl_prefix = upper.astype(BF16)
        incl_suffix = lower.astype(BF16)
        pf_ref[...] = (jnp.dot(h1, incl_prefix, preferred_element_type=F32)
                       + jnp.dot(h2, incl_prefix, preferred_element_type=F32)
                       + jnp.dot(h3, incl_prefix, preferred_element_type=F32))
        sf_ref[...] = (jnp.dot(h1, incl_suffix, preferred_element_type=F32)
                       + jnp.dot(h2, incl_suffix, preferred_element_type=F32)
                       + jnp.dot(h3, incl_suffix, preferred_element_type=F32))

    def trq(c, carry):
        cs = pl.ds(pl.multiple_of(c * L, L), L)
        qt_ref[:, cs] = (q_ref[0, cs, :].astype(F32) * (M_HEAD_DIM ** -0.5)).T
        va_ref[0:M_HEAD_DIM, cs] = v_ref[0, cs, :].astype(F32).T.astype(BF16)
        va_ref[M_HEAD_DIM:VA_ROWS, cs] = jnp.ones((ONES_ROWS, L), BF16)
        return carry

    lax.fori_loop(0, nc, trq, 0)

    for d, (_, _, cum_ref, i_off, f_off, g_lane, rev) in enumerate(dirs):
        li = gi_ref[pl.ds(pl.multiple_of((i_off + hd) * nc, 8), nc), :]
        bc = cum_ref[pl.ds(pl.multiple_of((f_off + hd) * nc, 8), nc), :]
        g = jnp.broadcast_to(bc[:, g_lane:g_lane + 1], (nc, LANES))
        a = g + (li - bc)
        m_loc = jnp.broadcast_to(jnp.max(a, axis=1, keepdims=True), (nc, LANES))
        w_ref[d] = jnp.exp2(a - m_loc)
        m = jnp.full((1, LANES), NEG, F32)
        for c in (range(nc - 1, -1, -1) if rev else range(nc)):
            mp_ref[d, c:c + 1, :] = m
            m = jnp.maximum(g[c:c + 1, :] + m, m_loc[c:c + 1, :])
            mq_ref[d, c:c + 1, :] = m
        so_ref[d] = jnp.exp2(g + mp_ref[d] - mq_ref[d])
        sn_ref[d] = jnp.exp2(m_loc - mq_ref[d])

    def inc_step(c, carry):
        cs = pl.ds(pl.multiple_of(c * L, L), L)
        vat = va_ref[:, cs].astype(F32)
        lhs = jnp.concatenate([vat * w_ref[0, pl.ds(c, 1), :],
                               vat * w_ref[1, pl.ds(c, 1), :]], axis=0).astype(BF16)
        dcn_ref[c] = jnp.dot(lhs, k_ref[0, cs, :], preferred_element_type=F32)
        return carry

    lax.fori_loop(0, nc, inc_step, 0, unroll=2)

    st_ref[...] = jnp.zeros_like(st_ref)

    def scan_step(i, carry):
        for d, (_, c_ref, _, _, _, _, rev) in enumerate(dirs):
            c = (nc - 1 - i) if rev else i
            st = st_ref[d]
            c_ref[c] = st.astype(BF16)
            st_ref[d] = (so_ref[d, pl.ds(c, 1), :] * st
                         + sn_ref[d, pl.ds(c, 1), :] * dcn_ref[c, d * VA_ROWS:(d + 1) * VA_ROWS, :])
        return carry

    lax.fori_loop(0, nc, scan_step, 0)

    def out_step(c, carry):
        cs = pl.ds(pl.multiple_of(c * L, L), L)
        kc = k_ref[0, cs, :]
        qt = qt_ref[:, cs]
        smt = jnp.dot(kc, qt.astype(BF16), preferred_element_type=F32)
        vat = va_ref[:, cs]
        hsum = jnp.zeros((M_HEAD_DIM, L), F32)
        for d, (mask, c_ref, cum_ref, i_off, f_off, _, _) in enumerate(dirs):
            cum = cum_ref[pl.ds((f_off + hd) * nc + c, 1), :]
            r = gi_ref[pl.ds((i_off + hd) * nc + c, 1), :] - cum
            mp = mp_ref[d, pl.ds(c, 1), :]
            r_col = jnp.sum(jnp.where(diag, r, 0.0), axis=1, keepdims=True)
            rm = jnp.where(mask, r_col, NEG)
            cm = jnp.maximum(jnp.max(rm, axis=0, keepdims=True), mp)
            wq = (jnp.exp2(rm - cm) * smt).astype(BF16)
            sq = (qt * jnp.exp2(mp - cm)).astype(BF16)
            y = jnp.dot(jnp.concatenate([vat, c_ref[c]], axis=1),
                        jnp.concatenate([wq, sq], axis=0), preferred_element_type=F32)
            floor = jnp.exp2(-(cum + cm))
            den = jnp.maximum(jnp.abs(y[M_HEAD_DIM:M_HEAD_DIM + 1, :]), floor)
            hsum = hsum + y[0:M_HEAD_DIM, :] * (1.0 / den)
        ms = jnp.sum(hsum * hsum, axis=0, keepdims=True) * (1.0 / M_HEAD_DIM)
        hn = (hsum * lax.rsqrt(ms + EPS)).T
        gate = jax.nn.sigmoid(o_ref[0, cs, :].astype(F32))
        out_ref[cs, :] = (hn * mn_ref[...] * gate).astype(BF16)
        return carry

    lax.fori_loop(0, nc, out_step, 0, unroll=4)


def _mlstm_call(z, g, bg, mn, batch, seq):
    n = batch * seq
    nc = seq // M_CHUNK
    dk = M_HEAD_DIM

    def zspec(base):
        return pl.BlockSpec((1, seq, LANES), lambda b, h: (base + h, b, 0))

    return pl.pallas_call(
        functools.partial(_mlstm_kernel, seq=seq),
        grid=(batch, M_HEADS),
        in_specs=[
            zspec(ZB_MQ), zspec(ZB_MK), zspec(ZB_MV), zspec(ZB_MO),
            pl.BlockSpec((seq, LANES), lambda b, h: (b, 0)),
            pl.BlockSpec((1, LANES), lambda b, h: (0, 0)),
            pl.BlockSpec((1, LANES), lambda b, h: (0, h)),
        ],
        out_specs=pl.BlockSpec((seq, LANES), lambda b, h: (b, h)),
        out_shape=jax.ShapeDtypeStruct((n, M_WIDTH), BF16),
        scratch_shapes=[
            pltpu.VMEM((nc * N_GATE, LANES), F32),
            pltpu.VMEM((nc * N_GATE, LANES), F32),
            pltpu.VMEM((nc * N_GATE, LANES), F32),
            pltpu.VMEM((nc * N_GATE, LANES), F32),
            pltpu.VMEM((dk, seq), F32),
            pltpu.VMEM((VA_ROWS, seq), BF16),
            pltpu.VMEM((2, nc, LANES), F32),
            pltpu.VMEM((2, nc, LANES), F32),
            pltpu.VMEM((2, nc, LANES), F32),
            pltpu.VMEM((2, nc, LANES), F32),
            pltpu.VMEM((2, nc, LANES), F32),
            pltpu.VMEM((nc, 2 * VA_ROWS, dk), F32),
            pltpu.VMEM((nc, VA_ROWS, dk), BF16),
            pltpu.VMEM((nc, VA_ROWS, dk), BF16),
            pltpu.VMEM((2, VA_ROWS, dk), F32),
        ],
        compiler_params=pltpu.CompilerParams(
            dimension_semantics=("arbitrary", "arbitrary"), vmem_limit_bytes=VMEM_LIMIT_BYTES),
        name="mlstm",
    )(z, z, z, z, g, bg, mn)


def _attn_kernel(sink_ref, q_ref, k_ref, v_ref, cos_ref, sin_ref, qn_ref, kn_ref, out_ref,
                 qs_ref, ks_ref, vs_ref, *, seq):
    W = WINDOW
    nb = seq // W
    kv = pl.program_id(1)
    lane = lax.broadcasted_iota(jnp.int32, (1, LANES), 1)
    first_half = (lane % A_HEAD_DIM) < (A_HEAD_DIM // 2)
    head_lo = lane < A_HEAD_DIM
    gr = lax.broadcasted_iota(jnp.int32, (LANES, LANES), 0) // A_HEAD_DIM
    gc = lax.broadcasted_iota(jnp.int32, (LANES, LANES), 1) // A_HEAD_DIM
    group_mean = jnp.where(gr == gc, 1.0 / A_HEAD_DIM, 0.0).astype(BF16)
    rows_blk = 256

    half = A_HEAD_DIM // 2
    pr = lax.broadcasted_iota(jnp.int32, (LANES, LANES), 0)
    pc = lax.broadcasted_iota(jnp.int32, (LANES, LANES), 1)
    pc_first = (pc % A_HEAD_DIM) < half
    rot_mat = jnp.where(pc_first & (pr == pc + half), -1.0,
                        jnp.where((~pc_first) & (pr == pc - half), 1.0, 0.0)).astype(BF16)

    def rot_gain(g_ref):
        g = jnp.broadcast_to(g_ref[...], (8, LANES))
        return jnp.where(first_half, pltpu.roll(g, LANES - half, 1), pltpu.roll(g, half, 1))[0:1, :]

    q_scale = (A_HEAD_DIM ** -0.5) * LOG2E
    qg, qg_rot = qn_ref[...] * q_scale, rot_gain(qn_ref) * q_scale
    kg, kg_rot = kn_ref[...], rot_gain(kn_ref)

    def norm_rope(xb, ct, st):
        x = xb.astype(F32)
        sq = x * x
        hi = sq.astype(BF16)
        lo = (sq - hi.astype(F32)).astype(BF16)
        ms = (jnp.dot(hi, group_mean, preferred_element_type=F32)
              + jnp.dot(lo, group_mean, preferred_element_type=F32))
        rx = jnp.dot(xb, rot_mat, preferred_element_type=F32)
        return ((x * ct + rx * st) * lax.rsqrt(ms + EPS)).astype(BF16)

    def prep(i, carry):
        rs = pl.ds(pl.multiple_of(i * rows_blk, rows_blk), rows_blk)
        cs = cos_ref[rs, :]
        sn = sin_ref[rs, :]
        qct, qst = cs * qg, sn * qg_rot
        for j in range(2):
            qs_ref[j, rs, :] = norm_rope(q_ref[j, rs, :], qct, qst)
        ks_ref[pl.ds(pl.multiple_of(W + i * rows_blk, W), rows_blk), :] = norm_rope(
            k_ref[0, rs, :], cs * kg, sn * kg_rot)
        return carry

    lax.fori_loop(0, seq // rows_blk, prep, 0)
    zpad = jnp.zeros((W, LANES), BF16)
    ks_ref[0:W, :] = zpad
    ks_ref[seq + W:seq + 2 * W, :] = zpad
    vs_ref[0:W, :] = zpad
    vs_ref[seq + W:seq + 2 * W, :] = zpad
    vs_ref[W:seq + W, :] = v_ref[0]

    qi = lax.broadcasted_iota(jnp.int32, (W, W), 0)
    kj = lax.broadcasted_iota(jnp.int32, (W, W), 1)
    left_band = jnp.where(kj >= qi, 0.0, NEG)
    right_band = jnp.where(kj <= qi, 0.0, NEG)
    zeros_mid = jnp.zeros((W, W), F32)

    def blk(n, carry):
        ws = pl.ds(pl.multiple_of(n * W, W), 3 * W)
        qrows = pl.ds(pl.multiple_of(n * W, W), W)
        kwin = ks_ref[ws, :]
        vwin = vs_ref[ws, :]
        q0 = qs_ref[0, qrows, :]
        q1 = qs_ref[1, qrows, :]
        zq = jnp.zeros_like(q0)
        qstack = jnp.concatenate([
            jnp.where(head_lo, q0, zq), jnp.where(head_lo, zq, q0),
            jnp.where(head_lo, q1, zq), jnp.where(head_lo, zq, q1)], axis=0)
        s = lax.dot_general(qstack, kwin, (((1,), (1,)), ((), ())),
                            preferred_element_type=F32)
        bias_l = left_band + jnp.where(n == 0, NEG, 0.0)
        bias_r = right_band + jnp.where(n == nb - 1, NEG, 0.0)
        outs = []
        for hh in range(A_GROUP):
            rows = slice(hh * W, (hh + 1) * W)
            sh = jnp.concatenate([s[rows, 0:W] + bias_l, s[rows, W:2 * W],
                                  s[rows, 2 * W:3 * W] + bias_r], axis=1)
            sk = sink_ref[kv * A_GROUP + hh] * LOG2E
            m = jnp.maximum(jnp.max(sh, axis=1, keepdims=True), sk)
            p = jnp.exp2(sh - m)
            den = jnp.sum(p, axis=1, keepdims=True) + jnp.exp2(sk - m)
            pv = jnp.dot(p.astype(BF16), vwin, preferred_element_type=F32)
            outs.append(pv / den)
        out_ref[qrows, 0:LANES] = jnp.where(head_lo, outs[0], outs[1]).astype(BF16)
        out_ref[qrows, LANES:2 * LANES] = jnp.where(head_lo, outs[2], outs[3]).astype(BF16)
        return carry

    lax.fori_loop(0, nb, blk, 0, unroll=2)


def _attn_call(z, sink, cos_t, sin_t, qn, kn, batch, seq):
    n = batch * seq
    return pl.pallas_call(
        functools.partial(_attn_kernel, seq=seq),
        grid=(batch, A_KV_HEADS),
        in_specs=[
            pl.BlockSpec(memory_space=pltpu.SMEM),
            pl.BlockSpec((2, seq, LANES), lambda b, g: (ZB_AQ // 2 + g, b, 0)),
            pl.BlockSpec((1, seq, LANES), lambda b, g: (ZB_AK + g, b, 0)),
            pl.BlockSpec((1, seq, LANES), lambda b, g: (ZB_AV + g, b, 0)),
            pl.BlockSpec((seq, LANES), lambda b, g: (0, 0)),
            pl.BlockSpec((seq, LANES), lambda b, g: (0, 0)),
            pl.BlockSpec((1, LANES), lambda b, g: (0, 0)),
            pl.BlockSpec((1, LANES), lambda b, g: (0, 0)),
        ],
        out_specs=pl.BlockSpec((seq, 2 * LANES), lambda b, g: (b, g)),
        out_shape=jax.ShapeDtypeStruct((n, A_WIDTH), BF16),
        scratch_shapes=[
            pltpu.VMEM((2, seq, LANES), BF16),
            pltpu.VMEM((seq + 2 * WINDOW, LANES), BF16),
            pltpu.VMEM((seq + 2 * WINDOW, LANES), BF16),
        ],
        compiler_params=pltpu.CompilerParams(
            dimension_semantics=("arbitrary", "arbitrary"), vmem_limit_bytes=VMEM_LIMIT_BYTES),
        name="window_attn",
    )(sink, z, z, z, cos_t, sin_t, qn, kn)


def _z_column_index():
    ak, av = 2576, 2704
    parts = [np.arange(0, 2048), np.arange(2064, 2576)]
    for base in (ak, av):
        for g in range(A_KV_HEADS):
            cols = np.arange(base + g * A_HEAD_DIM, base + (g + 1) * A_HEAD_DIM)
            parts += [cols, cols]
    parts.append(np.arange(2832, 4880))
    return np.concatenate(parts)


def _ffn_weights(w1, w3, w2):
    def cols(w):
        return w.astype(BF16).reshape(D_MODEL, N_FF_CHUNKS, FF_CHUNK).transpose(1, 0, 2)
    return cols(w1), cols(w3), w2.astype(BF16).reshape(N_FF_CHUNKS, FF_CHUNK, D_MODEL)


def _rope_tables(seq):
    half = A_HEAD_DIM // 2
    inv = jnp.power(ROPE_THETA, -jnp.arange(half, dtype=F32) / half)
    ang = jnp.arange(seq).astype(F32)[:, None] * inv[None, :]
    return jnp.tile(jnp.cos(ang), (1, LANES // half)), jnp.tile(jnp.sin(ang), (1, LANES // half))


def _layer(x, p):
    batch, seq, _ = x.shape
    xf = x.reshape(batch * seq, D_MODEL)
    x1, z, g = _front_call(xf, p["n1"], *p["ffn1"], p["nm"], p["wz"], p["wg"])
    hm = _mlstm_call(z, g, p["bg"], p["mn"], batch, seq)
    cos_t, sin_t = _rope_tables(seq)
    ha = _attn_call(z, p["sink"], cos_t, sin_t, p["qn"], p["kn"], batch, seq)
    y = _back_call(x1, z, hm, ha, p["wpm"], p["wpa"], p["wo"], p["n2"], *p["ffn2"])
    return y.reshape(batch, seq, D_MODEL)


def kernel(x_prompt, x_sample, ffn1_norm, ffn1_w1, ffn1_w3, ffn1_w2, mix_norm, w_in, b_gates, m_norm, q_norm, k_norm, sink, w_pm, w_pa, w_out, ffn2_norm, ffn2_w1, ffn2_w3, ffn2_w2):
    depth = w_in.shape[0]
    zidx = _z_column_index()
    y_prompt, y_sample = x_prompt, x_sample
    for l in range(depth):
        wz = w_in[l][:, zidx].astype(BF16)
        p = {
            "n1": ffn1_norm[l][None, :],
            "ffn1": _ffn_weights(ffn1_w1[l], ffn1_w3[l], ffn1_w2[l]),
            "nm": mix_norm[l][None, :],
            "wz": wz.reshape(D_MODEL, N_Z_CHUNKS, Z_CHUNK).transpose(1, 0, 2),
            "wg": jnp.pad(w_in[l][:, 2048:2048 + N_GATE], ((0, 0), (0, LANES - N_GATE))).astype(BF16),
            "bg": jnp.pad(b_gates[l], (0, LANES - N_GATE))[None, :],
            "mn": m_norm[l][None, :],
            "qn": jnp.tile(q_norm[l], LANES // A_HEAD_DIM)[None, :],
            "kn": jnp.tile(k_norm[l], LANES // A_HEAD_DIM)[None, :],
            "sink": sink[l],
            "wpm": w_pm[l].astype(BF16),
            "wpa": w_pa[l].astype(BF16),
            "wo": w_out[l].astype(BF16),
            "n2": ffn2_norm[l][None, :],
            "ffn2": _ffn_weights(ffn2_w1[l], ffn2_w3[l], ffn2_w2[l]),
        }
        y_prompt = _layer(y_prompt, p)
        y_sample = _layer(y_sample, p)
    return (y_prompt, y_sample)
```

```python
import functools

import numpy as np
import jax
import jax.numpy as jnp
from jax import lax
from jax.experimental import pallas as pl
from jax.experimental.pallas import tpu as pltpu

F32 = jnp.float32
BF16 = jnp.bfloat16

D_MODEL = 1024
D_FF = 2816
M_HEADS = 4
M_HEAD_DIM = 128
M_WIDTH = 512
N_GATE = 16
A_HEADS = 8
A_KV_HEADS = 2
A_GROUP = 4
A_HEAD_DIM = 64
A_WIDTH = 512
WINDOW = 128
ROPE_THETA = 10000.0
EPS = 1e-6
NEG = -1e30
LOG2E = 1.4426950408889634

LANES = 128
FF_CHUNK = 256
N_FF_CHUNKS = D_FF // FF_CHUNK
Z_CHUNK = 512
Z_COLS = 5120
N_Z_CHUNKS = Z_COLS // Z_CHUNK
N_T_BLOCKS = 12
N_Z_BLOCKS = Z_COLS // LANES - N_T_BLOCKS
TB_MQ, TB_MV, TB_MO = 0, 4, 8
ZB_GM, ZB_GA, ZB_MK, ZB_AQ, ZB_AK, ZB_AV = 0, 8, 16, 20, 24, 26
M_CHUNK = 128
ONES_ROWS = 16
VA_ROWS = M_HEAD_DIM + ONES_ROWS
OUT_GROUP = 8
ATTN_GROUP = 2
TOKEN_TILE = 512
VMEM_LIMIT_BYTES = 60 * 1024 * 1024


def _rms(x, g):
    return x * lax.rsqrt(jnp.mean(x * x, axis=-1, keepdims=True) + EPS) * g


def _const_spec(shape):
    nd = len(shape)
    return pl.BlockSpec(shape, lambda *_: (0,) * nd, pipeline_mode=pl.Buffered(1))


def _swiglu_into(h_ref, w1_ref, w3_ref, w2_ref, acc_ref):
    for c in range(N_FF_CHUNKS):
        h = h_ref[...]
        a = jnp.dot(h, w1_ref[c], preferred_element_type=F32)
        b = jnp.dot(h, w3_ref[c], preferred_element_type=F32)
        hid = (a * jax.nn.sigmoid(a) * b).astype(BF16)
        part = jnp.dot(hid, w2_ref[c], preferred_element_type=F32)
        if c == 0:
            acc_ref[...] = part
        else:
            acc_ref[...] += part


def _front_kernel(x_ref, n1_ref, w1_ref, w3_ref, w2_ref, nm_ref, wz_ref, wg_ref,
                  x1_ref, z_ref, zt_ref, gt_ref, h_ref, acc_ref):
    x = x_ref[...]
    h_ref[...] = _rms(x, n1_ref[...]).astype(BF16)
    _swiglu_into(h_ref, w1_ref, w3_ref, w2_ref, acc_ref)
    x1 = x + 0.5 * acc_ref[...]
    x1_ref[...] = x1
    h_ref[...] = _rms(x1, nm_ref[...]).astype(BF16)

    for j in range(N_Z_CHUNKS):
        z = jnp.dot(h_ref[...], wz_ref[j], preferred_element_type=F32)
        for q in range(Z_CHUNK // LANES):
            blk = j * (Z_CHUNK // LANES) + q
            zb = z[:, q * LANES:(q + 1) * LANES]
            if blk < TB_MV:
                zt_ref[blk] = (zb * (M_HEAD_DIM ** -0.5)).T.astype(BF16)
            elif blk < N_T_BLOCKS:
                zt_ref[blk] = zb.T.astype(BF16)
            else:
                z_ref[blk - N_T_BLOCKS] = zb.astype(BF16)
    g = jnp.dot(h_ref[...], wg_ref[...], preferred_element_type=F32)
    gt_ref[...] = g.T[0:N_GATE, :]


def _front_call(x, n1, w1, w3, w2, nm, wz, wg):
    n = x.shape[0]
    tm = TOKEN_TILE
    return pl.pallas_call(
        _front_kernel,
        grid=(n // tm,),
        in_specs=[
            pl.BlockSpec((tm, D_MODEL), lambda i: (i, 0)),
            _const_spec((1, D_MODEL)),
            _const_spec((N_FF_CHUNKS, D_MODEL, FF_CHUNK)),
            _const_spec((N_FF_CHUNKS, D_MODEL, FF_CHUNK)),
            _const_spec((N_FF_CHUNKS, FF_CHUNK, D_MODEL)),
            _const_spec((1, D_MODEL)),
            _const_spec((N_Z_CHUNKS, D_MODEL, Z_CHUNK)),
            _const_spec((D_MODEL, LANES)),
        ],
        out_specs=[
            pl.BlockSpec((tm, D_MODEL), lambda i: (i, 0)),
            pl.BlockSpec((N_Z_BLOCKS, tm, LANES), lambda i: (0, i, 0)),
            pl.BlockSpec((N_T_BLOCKS, LANES, tm), lambda i: (0, 0, i)),
            pl.BlockSpec((N_GATE, tm), lambda i: (0, i)),
        ],
        out_shape=[
            jax.ShapeDtypeStruct((n, D_MODEL), F32),
            jax.ShapeDtypeStruct((N_Z_BLOCKS, n, LANES), BF16),
            jax.ShapeDtypeStruct((N_T_BLOCKS, LANES, n), BF16),
            jax.ShapeDtypeStruct((N_GATE, n), F32),
        ],
        scratch_shapes=[
            pltpu.VMEM((tm, D_MODEL), BF16),
            pltpu.VMEM((tm, D_MODEL), F32),
        ],
        compiler_params=pltpu.CompilerParams(
            dimension_semantics=("arbitrary",), vmem_limit_bytes=VMEM_LIMIT_BYTES),
        name="front_ffn_proj",
    )(x, n1, w1, w3, w2, nm, wz, wg)


def _back_kernel(x1_ref, gm_ref, ga_ref, hm_ref, ha_ref, wpm_ref, wpa_ref, wo_ref,
                 n2_ref, w1_ref, w3_ref, w2_ref, y_ref, h_ref, acc_ref):
    pm = lax.dot_general(hm_ref[...], wpm_ref[...], (((0,), (0,)), ((), ())),
                         preferred_element_type=F32)
    pa = jnp.dot(ha_ref[...], wpa_ref[...], preferred_element_type=F32)
    for j in range(D_MODEL // LANES):
        sl = slice(j * LANES, (j + 1) * LANES)
        gm = jax.nn.sigmoid(gm_ref[j].astype(F32))
        ga = jax.nn.sigmoid(ga_ref[j].astype(F32))
        h_ref[:, sl] = (gm * pm[:, sl] + ga * pa[:, sl]).astype(BF16)
    x2 = x1_ref[...] + jnp.dot(h_ref[...], wo_ref[...], preferred_element_type=F32)
    h_ref[...] = _rms(x2, n2_ref[...]).astype(BF16)
    _swiglu_into(h_ref, w1_ref, w3_ref, w2_ref, acc_ref)
    y_ref[...] = x2 + 0.5 * acc_ref[...]


def _back_call(x1, z, hm, ha, wpm, wpa, wo, n2, w1, w3, w2):
    n = x1.shape[0]
    tm = TOKEN_TILE
    gblk = D_MODEL // LANES
    return pl.pallas_call(
        _back_kernel,
        grid=(n // tm,),
        in_specs=[
            pl.BlockSpec((tm, D_MODEL), lambda i: (i, 0)),
            pl.BlockSpec((gblk, tm, LANES), lambda i: (ZB_GM // gblk, i, 0)),
            pl.BlockSpec((gblk, tm, LANES), lambda i: (ZB_GA // gblk, i, 0)),
            pl.BlockSpec((M_WIDTH, tm), lambda i: (0, i)),
            pl.BlockSpec((tm, A_WIDTH), lambda i: (i, 0)),
            _const_spec((M_WIDTH, D_MODEL)),
            _const_spec((A_WIDTH, D_MODEL)),
            _const_spec((D_MODEL, D_MODEL)),
            _const_spec((1, D_MODEL)),
            _const_spec((N_FF_CHUNKS, D_MODEL, FF_CHUNK)),
            _const_spec((N_FF_CHUNKS, D_MODEL, FF_CHUNK)),
            _const_spec((N_FF_CHUNKS, FF_CHUNK, D_MODEL)),
        ],
        out_specs=pl.BlockSpec((tm, D_MODEL), lambda i: (i, 0)),
        out_shape=jax.ShapeDtypeStruct((n, D_MODEL), F32),
        scratch_shapes=[
            pltpu.VMEM((tm, D_MODEL), BF16),
            pltpu.VMEM((tm, D_MODEL), F32),
        ],
        compiler_params=pltpu.CompilerParams(
            dimension_semantics=("arbitrary",), vmem_limit_bytes=VMEM_LIMIT_BYTES),
        name="back_merge_ffn",
    )(x1, z, z, hm, ha, wpm, wpa, wo, n2, w1, w3, w2)


def _split3(x):
    h1 = x.astype(BF16)
    r1 = x - h1.astype(F32)
    h2 = r1.astype(BF16)
    h3 = (r1 - h2.astype(F32)).astype(BF16)
    return h1, h2, h3


def _mlstm_kernel(qt_ref, k_ref, vt_ref, ot_ref, g_ref, bg_ref, mn_ref, out_ref,
                  gi_ref, lf_ref, pf_ref, sf_ref, w_ref, mp_ref, mq_ref,
                  so_ref, sn_ref, dcn_ref, cf_ref, cb_ref, st_ref, *, seq):
    L = M_CHUNK
    nc = seq // L
    hd = pl.program_id(1)
    row = lax.broadcasted_iota(jnp.int32, (L, L), 0)
    col = lax.broadcasted_iota(jnp.int32, (L, L), 1)
    lower = col <= row
    upper = col >= row
    diag = col == row
    dirs = ((upper, cf_ref, pf_ref, 0, 4, L - 1, False),
            (lower, cb_ref, sf_ref, 8, 12, 0, True))

    @pl.when(hd == 0)
    def _():
        gt = g_ref[...].reshape(N_GATE * nc, LANES) + bg_ref[...]
        gi_ref[...] = gt * LOG2E
        lf_ref[...] = (jnp.minimum(gt, 0.0) - jnp.log1p(jnp.exp(-jnp.abs(gt)))) * LOG2E
        h1, h2, h3 = _split3(lf_ref[...])
        incl_prefix = upper.astype(BF16)
        incl_suffix = lower.astype(BF16)
        pf_ref[...] = (jnp.dot(h1, incl_prefix, preferred_element_type=F32)
                       + jnp.dot(h2, incl_prefix, preferred_element_type=F32)
                       + jnp.dot(h3, incl_prefix, preferred_element_type=F32))
        sf_ref[...] = (jnp.dot(h1, incl_suffix, preferred_element_type=F32)
                       + jnp.dot(h2, incl_suffix, preferred_element_type=F32)
                       + jnp.dot(h3, incl_suffix, preferred_element_type=F32))

    ones_rows = jnp.ones((ONES_ROWS, L), BF16)

    def v_aug(cs):
        return jnp.concatenate([vt_ref[0, :, cs], ones_rows], axis=0)

    for d, (_, _, cum_ref, i_off, f_off, g_lane, rev) in enumerate(dirs):
        li = gi_ref[pl.ds(pl.multiple_of((i_off + hd) * nc, 8), nc), :]
        bc = cum_ref[pl.ds(pl.multiple_of((f_off + hd) * nc, 8), nc), :]
        g = jnp.broadcast_to(bc[:, g_lane:g_lane + 1], (nc, LANES))
        a = g + (li - bc)
        m_loc = jnp.broadcast_to(jnp.max(a, axis=1, keepdims=True), (nc, LANES))
        w_ref[d] = jnp.exp2(a - m_loc)
        m = jnp.full((1, LANES), NEG, F32)
        for c in (range(nc - 1, -1, -1) if rev else range(nc)):
            mp_ref[d, c:c + 1, :] = m
            m = jnp.maximum(g[c:c + 1, :] + m, m_loc[c:c + 1, :])
            mq_ref[d, c:c + 1, :] = m
        so_ref[d] = jnp.exp2(g + mp_ref[d] - mq_ref[d])
        sn_ref[d] = jnp.exp2(m_loc - mq_ref[d])

    def inc_group(i, carry):
        chunks = [i * OUT_GROUP + g for g in range(OUT_GROUP)]
        spans = [pl.ds(pl.multiple_of(c * L, L), L) for c in chunks]
        lhs = []
        for c, cs in zip(chunks, spans):
            vat = v_aug(cs).astype(F32)
            lhs.append(jnp.concatenate([vat * w_ref[0, pl.ds(c, 1), :],
                                        vat * w_ref[1, pl.ds(c, 1), :]], axis=0).astype(BF16))
        for c, cs, lh in zip(chunks, spans, lhs):
            dcn_ref[c] = jnp.dot(lh, k_ref[0, cs, :], preferred_element_type=F32)
        return carry

    lax.fori_loop(0, nc // OUT_GROUP, inc_group, 0)

    st_ref[...] = jnp.zeros_like(st_ref)

    def scan_step(i, carry):
        for d, (_, c_ref, _, _, _, _, rev) in enumerate(dirs):
            c = (nc - 1 - i) if rev else i
            st = st_ref[d]
            c_ref[c] = st.astype(BF16)
            st_ref[d] = (so_ref[d, pl.ds(c, 1), :] * st
                         + sn_ref[d, pl.ds(c, 1), :] * dcn_ref[c, d * VA_ROWS:(d + 1) * VA_ROWS, :])
        return carry

    lax.fori_loop(0, nc, scan_step, 0)

    def out_group(i, carry):
        chunks = [i * OUT_GROUP + g for g in range(OUT_GROUP)]
        spans = [pl.ds(pl.multiple_of(c * L, L), L) for c in chunks]
        qtb = [qt_ref[0, :, cs] for cs in spans]
        smt = [jnp.dot(k_ref[0, cs, :], q, preferred_element_type=F32)
               for cs, q in zip(spans, qtb)]
        rhs, floor = [], []
        for c, q, sm in zip(chunks, qtb, smt):
            for d, (mask, _, cum_ref, i_off, f_off, _, _) in enumerate(dirs):
                cum = cum_ref[pl.ds((f_off + hd) * nc + c, 1), :]
                r = gi_ref[pl.ds((i_off + hd) * nc + c, 1), :] - cum
                mp = mp_ref[d, pl.ds(c, 1), :]
                r_col = jnp.sum(jnp.where(diag, r, 0.0), axis=1, keepdims=True)
                rm = jnp.where(mask, r_col, NEG)
                cm = jnp.maximum(jnp.max(rm, axis=0, keepdims=True), mp)
                wq = (jnp.exp2(rm - cm) * sm).astype(BF16)
                sq = q * jnp.exp2(mp - cm).astype(BF16)
                rhs.append(jnp.concatenate([wq, sq], axis=0))
                floor.append(jnp.exp2(-(cum + cm)))
        ys = []
        for g, (c, cs) in enumerate(zip(chunks, spans)):
            vat = v_aug(cs)
            for d, (_, c_ref, _, _, _, _, _) in enumerate(dirs):
                ys.append(jnp.dot(jnp.concatenate([vat, c_ref[c]], axis=1), rhs[2 * g + d],
                                  preferred_element_type=F32))
        for g, cs in enumerate(spans):
            hsum = jnp.zeros((M_HEAD_DIM, L), F32)
            for d in range(2):
                y = ys[2 * g + d]
                den = jnp.maximum(jnp.abs(y[M_HEAD_DIM:M_HEAD_DIM + 1, :]), floor[2 * g + d])
                hsum = hsum + y[0:M_HEAD_DIM, :] * (1.0 / den)
            ms = jnp.sum(hsum * hsum, axis=0, keepdims=True) * (1.0 / M_HEAD_DIM)
            hn = hsum * lax.rsqrt(ms + EPS)
            gate = jax.nn.sigmoid(ot_ref[0, :, cs].astype(F32))
            out_ref[:, cs] = (hn * mn_ref[...] * gate).astype(BF16)
        return carry

    lax.fori_loop(0, nc // OUT_GROUP, out_group, 0)


def _mlstm_call(z, zt, g, bg, mn, batch, seq):
    n = batch * seq
    nc = seq // M_CHUNK
    dk = M_HEAD_DIM

    def tspec(base):
        return pl.BlockSpec((1, LANES, seq), lambda b, h: (base + h, 0, b))

    return pl.pallas_call(
        functools.partial(_mlstm_kernel, seq=seq),
        grid=(batch, M_HEADS),
        in_specs=[
            tspec(TB_MQ),
            pl.BlockSpec((1, seq, LANES), lambda b, h: (ZB_MK + h, b, 0)),
            tspec(TB_MV), tspec(TB_MO),
            pl.BlockSpec((N_GATE, 1, nc, LANES), lambda b, h: (0, b, 0, 0)),
            pl.BlockSpec((N_GATE * nc, LANES), lambda b, h: (0, 0)),
            pl.BlockSpec((dk, LANES), lambda b, h: (h, 0)),
        ],
        out_specs=pl.BlockSpec((dk, seq), lambda b, h: (h, b)),
        out_shape=jax.ShapeDtypeStruct((M_WIDTH, n), BF16),
        scratch_shapes=[
            pltpu.VMEM((nc * N_GATE, LANES), F32),
            pltpu.VMEM((nc * N_GATE, LANES), F32),
            pltpu.VMEM((nc * N_GATE, LANES), F32),
            pltpu.VMEM((nc * N_GATE, LANES), F32),
            pltpu.VMEM((2, nc, LANES), F32),
            pltpu.VMEM((2, nc, LANES), F32),
            pltpu.VMEM((2, nc, LANES), F32),
            pltpu.VMEM((2, nc, LANES), F32),
            pltpu.VMEM((2, nc, LANES), F32),
            pltpu.VMEM((nc, 2 * VA_ROWS, dk), F32),
            pltpu.VMEM((nc, VA_ROWS, dk), BF16),
            pltpu.VMEM((nc, VA_ROWS, dk), BF16),
            pltpu.VMEM((2, VA_ROWS, dk), F32),
        ],
        compiler_params=pltpu.CompilerParams(
            dimension_semantics=("arbitrary", "arbitrary"), vmem_limit_bytes=VMEM_LIMIT_BYTES),
        name="mlstm",
    )(zt, z, zt, zt, g, bg, mn)


def _attn_kernel(sink_ref, q_ref, k_ref, v_ref, cos_ref, sin_ref, qn_ref, kn_ref, out_ref,
                 qs_ref, ks_ref, vs_ref, *, seq):
    W = WINDOW
    nb = seq // W
    kv = pl.program_id(1)
    lane = lax.broadcasted_iota(jnp.int32, (1, LANES), 1)
    first_half = (lane % A_HEAD_DIM) < (A_HEAD_DIM // 2)
    head_lo = lane < A_HEAD_DIM
    gr = lax.broadcasted_iota(jnp.int32, (LANES, LANES), 0) // A_HEAD_DIM
    gc = lax.broadcasted_iota(jnp.int32, (LANES, LANES), 1) // A_HEAD_DIM
    group_mean = jnp.where(gr == gc, 1.0 / A_HEAD_DIM, 0.0).astype(BF16)
    rows_blk = 256

    half = A_HEAD_DIM // 2
    pr = lax.broadcasted_iota(jnp.int32, (LANES, LANES), 0)
    pc = lax.broadcasted_iota(jnp.int32, (LANES, LANES), 1)
    pc_first = (pc % A_HEAD_DIM) < half
    rot_mat = jnp.where(pc_first & (pr == pc + half), -1.0,
                        jnp.where((~pc_first) & (pr == pc - half), 1.0, 0.0)).astype(BF16)

    def rot_gain(g_ref):
        g = jnp.broadcast_to(g_ref[...], (8, LANES))
        return jnp.where(first_half, pltpu.roll(g, LANES - half, 1), pltpu.roll(g, half, 1))[0:1, :]

    q_scale = (A_HEAD_DIM ** -0.5) * LOG2E
    qg, qg_rot = qn_ref[...] * q_scale, rot_gain(qn_ref) * q_scale
    kg, kg_rot = kn_ref[...], rot_gain(kn_ref)

    def norm_rope(xb, ct, st):
        x = xb.astype(F32)
        sq = x * x
        hi = sq.astype(BF16)
        lo = (sq - hi.astype(F32)).astype(BF16)
        ms = (jnp.dot(hi, group_mean, preferred_element_type=F32)
              + jnp.dot(lo, group_mean, preferred_element_type=F32))
        rx = jnp.dot(xb, rot_mat, preferred_element_type=F32)
        return ((x * ct + rx * st) * lax.rsqrt(ms + EPS)).astype(BF16)

    def prep(i, carry):
        rs = pl.ds(pl.multiple_of(i * rows_blk, rows_blk), rows_blk)
        cs = cos_ref[rs, :]
        sn = sin_ref[rs, :]
        qct, qst = cs * qg, sn * qg_rot
        for j in range(2):
            qs_ref[j, rs, :] = norm_rope(q_ref[j, rs, :], qct, qst)
        ks_ref[pl.ds(pl.multiple_of(W + i * rows_blk, W), rows_blk), :] = norm_rope(
            k_ref[0, rs, :], cs * kg, sn * kg_rot)
        return carry

    lax.fori_loop(0, seq // rows_blk, prep, 0)
    zpad = jnp.zeros((W, LANES), BF16)
    ks_ref[0:W, :] = zpad
    ks_ref[seq + W:seq + 2 * W, :] = zpad
    vs_ref[0:W, :] = zpad
    vs_ref[seq + W:seq + 2 * W, :] = zpad
    vs_ref[W:seq + W, :] = v_ref[0]

    qi = lax.broadcasted_iota(jnp.int32, (W, W), 0)
    kj = lax.broadcasted_iota(jnp.int32, (W, W), 1)
    left_band = jnp.where(kj >= qi, 0.0, NEG)
    right_band = jnp.where(kj <= qi, 0.0, NEG)
    zeros_mid = jnp.zeros((W, W), F32)

    def blk_group(i, carry):
        blocks = [i * ATTN_GROUP + j for j in range(ATTN_GROUP)]
        qrows = [pl.ds(pl.multiple_of(n * W, W), W) for n in blocks]
        wins = [pl.ds(pl.multiple_of(n * W, W), 3 * W) for n in blocks]
        scores = []
        for qr, ws in zip(qrows, wins):
            q0 = qs_ref[0, qr, :]
            q1 = qs_ref[1, qr, :]
            zq = jnp.zeros_like(q0)
            qstack = jnp.concatenate([
                jnp.where(head_lo, q0, zq), jnp.where(head_lo, zq, q0),
                jnp.where(head_lo, q1, zq), jnp.where(head_lo, zq, q1)], axis=0)
            scores.append(lax.dot_general(qstack, ks_ref[ws, :], (((1,), (1,)), ((), ())),
                                          preferred_element_type=F32))
        probs, dens = [], []
        for n, s in zip(blocks, scores):
            bias_l = left_band + jnp.where(n == 0, NEG, 0.0)
            bias_r = right_band + jnp.where(n == nb - 1, NEG, 0.0)
            for hh in range(A_GROUP):
                rows = slice(hh * W, (hh + 1) * W)
                sh = jnp.concatenate([s[rows, 0:W] + bias_l, s[rows, W:2 * W],
                                      s[rows, 2 * W:3 * W] + bias_r], axis=1)
                sk = sink_ref[kv * A_GROUP + hh] * LOG2E
                m = jnp.maximum(jnp.max(sh, axis=1, keepdims=True), sk)
                p = jnp.exp2(sh - m)
                dens.append(jnp.sum(p, axis=1, keepdims=True) + jnp.exp2(sk - m))
                probs.append(p.astype(BF16))
        pvs = [jnp.dot(p, vs_ref[wins[j // A_GROUP], :], preferred_element_type=F32)
               for j, p in enumerate(probs)]
        for b, qr in enumerate(qrows):
            o = [pvs[b * A_GROUP + hh] / dens[b * A_GROUP + hh] for hh in range(A_GROUP)]
            out_ref[qr, 0:LANES] = jnp.where(head_lo, o[0], o[1]).astype(BF16)
            out_ref[qr, LANES:2 * LANES] = jnp.where(head_lo, o[2], o[3]).astype(BF16)
        return carry

    lax.fori_loop(0, nb // ATTN_GROUP, blk_group, 0)


def _attn_call(z, sink, cos_t, sin_t, qn, kn, batch, seq):
    n = batch * seq
    return pl.pallas_call(
        functools.partial(_attn_kernel, seq=seq),
        grid=(batch, A_KV_HEADS),
        in_specs=[
            pl.BlockSpec(memory_space=pltpu.SMEM),
            pl.BlockSpec((2, seq, LANES), lambda b, g: (ZB_AQ // 2 + g, b, 0)),
            pl.BlockSpec((1, seq, LANES), lambda b, g: (ZB_AK + g, b, 0)),
            pl.BlockSpec((1, seq, LANES), lambda b, g: (ZB_AV + g, b, 0)),
            pl.BlockSpec((seq, LANES), lambda b, g: (0, 0)),
            pl.BlockSpec((seq, LANES), lambda b, g: (0, 0)),
            pl.BlockSpec((1, LANES), lambda b, g: (0, 0)),
            pl.BlockSpec((1, LANES), lambda b, g: (0, 0)),
        ],
        out_specs=pl.BlockSpec((seq, 2 * LANES), lambda b, g: (b, g)),
        out_shape=jax.ShapeDtypeStruct((n, A_WIDTH), BF16),
        scratch_shapes=[
            pltpu.VMEM((2, seq, LANES), BF16),
            pltpu.VMEM((seq + 2 * WINDOW, LANES), BF16),
            pltpu.VMEM((seq + 2 * WINDOW, LANES), BF16),
        ],
        compiler_params=pltpu.CompilerParams(
            dimension_semantics=("arbitrary", "arbitrary"), vmem_limit_bytes=VMEM_LIMIT_BYTES),
        name="window_attn",
    )(sink, z, z, z, cos_t, sin_t, qn, kn)


def _z_column_index():
    mq, mk, mv, mo, aq, ak, av, gm = 0, 512, 1024, 1536, 2064, 2576, 2704, 2832
    parts = [np.arange(mq, mq + M_WIDTH), np.arange(mv, mv + M_WIDTH), np.arange(mo, mo + M_WIDTH),
             np.arange(gm, gm + 2 * D_MODEL), np.arange(mk, mk + M_WIDTH), np.arange(aq, aq + A_WIDTH)]
    for base in (ak, av):
        for g in range(A_KV_HEADS):
            cols = np.arange(base + g * A_HEAD_DIM, base + (g + 1) * A_HEAD_DIM)
            parts += [cols, cols]
    return np.concatenate(parts)


def _ffn_weights(w1, w3, w2):
    def cols(w):
        return w.astype(BF16).reshape(D_MODEL, N_FF_CHUNKS, FF_CHUNK).transpose(1, 0, 2)
    return cols(w1), cols(w3), w2.astype(BF16).reshape(N_FF_CHUNKS, FF_CHUNK, D_MODEL)


def _rope_tables(seq):
    half = A_HEAD_DIM // 2
    inv = jnp.power(ROPE_THETA, -jnp.arange(half, dtype=F32) / half)
    ang = jnp.arange(seq).astype(F32)[:, None] * inv[None, :]
    return jnp.tile(jnp.cos(ang), (1, LANES // half)), jnp.tile(jnp.sin(ang), (1, LANES // half))


def _layer(x, p):
    batch, seq, _ = x.shape
    xf = x.reshape(batch * seq, D_MODEL)
    nc = seq // M_CHUNK
    x1, z, zt, gt = _front_call(xf, p["n1"], *p["ffn1"], p["nm"], p["wz"], p["wg"])
    hm = _mlstm_call(z, zt, gt.reshape(N_GATE, batch, nc, M_CHUNK),
                     jnp.broadcast_to(jnp.repeat(p["bg"], nc)[:, None], (N_GATE * nc, LANES)),
                     p["mn"], batch, seq)
    cos_t, sin_t = _rope_tables(seq)
    ha = _attn_call(z, p["sink"], cos_t, sin_t, p["qn"], p["kn"], batch, seq)
    y = _back_call(x1, z, hm, ha, p["wpm"], p["wpa"], p["wo"], p["n2"], *p["ffn2"])
    return y.reshape(batch, seq, D_MODEL)


def kernel(x_prompt, x_sample, ffn1_norm, ffn1_w1, ffn1_w3, ffn1_w2, mix_norm, w_in, b_gates, m_norm, q_norm, k_norm, sink, w_pm, w_pa, w_out, ffn2_norm, ffn2_w1, ffn2_w3, ffn2_w2):
    depth = w_in.shape[0]
    zidx = _z_column_index()
    y_prompt, y_sample = x_prompt, x_sample
    for l in range(depth):
        wz = w_in[l][:, zidx].astype(BF16)
        p = {
            "n1": ffn1_norm[l][None, :],
            "ffn1": _ffn_weights(ffn1_w1[l], ffn1_w3[l], ffn1_w2[l]),
            "nm": mix_norm[l][None, :],
            "wz": wz.reshape(D_MODEL, N_Z_CHUNKS, Z_CHUNK).transpose(1, 0, 2),
            "wg": jnp.pad(w_in[l][:, 2048:2048 + N_GATE], ((0, 0), (0, LANES - N_GATE))).astype(BF16),
            "bg": b_gates[l],
            "mn": jnp.broadcast_to(m_norm[l][:, None], (M_WIDTH, LANES)),
            "qn": jnp.tile(q_norm[l], LANES // A_HEAD_DIM)[None, :],
            "kn": jnp.tile(k_norm[l], LANES // A_HEAD_DIM)[None, :],
            "sink": sink[l],
            "wpm": w_pm[l].astype(BF16),
            "wpa": w_pa[l].astype(BF16),
            "wo": w_out[l].astype(BF16),
            "n2": ffn2_norm[l][None, :],
            "ffn2": _ffn_weights(ffn2_w1[l], ffn2_w3[l], ffn2_w2[l]),
        }
        y_prompt = _layer(y_prompt, p)
        y_sample = _layer(y_sample, p)
    return (y_prompt, y_sample)
```

```python
import functools

import numpy as np
import jax
import jax.numpy as jnp
from jax import lax
from jax.experimental import pallas as pl
from jax.experimental.pallas import tpu as pltpu

F32 = jnp.float32
BF16 = jnp.bfloat16

D_MODEL = 1024
D_FF = 2816
M_HEADS = 4
M_HEAD_DIM = 128
M_WIDTH = 512
N_GATE = 16
A_HEADS = 8
A_KV_HEADS = 2
A_GROUP = 4
A_HEAD_DIM = 64
A_WIDTH = 512
WINDOW = 128
ROPE_THETA = 10000.0
EPS = 1e-6
NEG = -1e30
LOG2E = 1.4426950408889634

LANES = 128
FF_CHUNK = 256
N_FF_CHUNKS = D_FF // FF_CHUNK
Z_CHUNK = 512
Z_COLS = 5120
N_Z_CHUNKS = Z_COLS // Z_CHUNK
N_T_BLOCKS = 12
N_Z_BLOCKS = Z_COLS // LANES - N_T_BLOCKS
TB_MQ, TB_MV, TB_MO = 0, 4, 8
ZB_GM, ZB_GA, ZB_MK, ZB_AQ, ZB_AK, ZB_AV = 0, 8, 16, 20, 24, 26
M_CHUNK = 128
ONES_ROWS = 16
VA_ROWS = M_HEAD_DIM + ONES_ROWS
OUT_GROUP = 8
ATTN_GROUP = 2
TOKEN_TILE = 512
SUB_TILES = 2
VMEM_LIMIT_BYTES = 60 * 1024 * 1024


def _rms(x, g):
    return x * lax.rsqrt(jnp.mean(x * x, axis=-1, keepdims=True) + EPS) * g


def _const_spec(shape):
    nd = len(shape)
    return pl.BlockSpec(shape, lambda *_: (0,) * nd, pipeline_mode=pl.Buffered(1))


def _swiglu_into(h_ref, w1_ref, w3_ref, w2_ref, acc_ref):
    def up(c):
        cols = slice(c * FF_CHUNK, (c + 1) * FF_CHUNK)
        h = h_ref[...]
        return (jnp.dot(h, w1_ref[:, cols], preferred_element_type=F32),
                jnp.dot(h, w3_ref[:, cols], preferred_element_type=F32))

    nxt = up(0)
    for c in range(N_FF_CHUNKS):
        a, b = nxt
        if c + 1 < N_FF_CHUNKS:
            nxt = up(c + 1)
        hid = (a * jax.nn.sigmoid(a) * b).astype(BF16)
        part = jnp.dot(hid, w2_ref[c * FF_CHUNK:(c + 1) * FF_CHUNK, :],
                       preferred_element_type=F32)
        if c == 0:
            acc_ref[...] = part
        else:
            acc_ref[...] += part


def _front_kernel(x_ref, n1_ref, w1_ref, w3_ref, w2_ref, nm_ref, wz_ref, wg_ref,
                  x1_ref, z_ref, zt_ref, gt_ref, *scratch):
    hs, h2s, accs = (scratch[k * SUB_TILES:(k + 1) * SUB_TILES] for k in range(3))
    sub = x_ref.shape[0] // SUB_TILES
    rows = [slice(s * sub, (s + 1) * sub) for s in range(SUB_TILES)]

    for s in range(SUB_TILES):
        hs[s][...] = _rms(x_ref[rows[s], :], n1_ref[...]).astype(BF16)
    for s in range(SUB_TILES):
        _swiglu_into(hs[s], w1_ref, w3_ref, w2_ref, accs[s])
    for s in range(SUB_TILES):
        x1 = x_ref[rows[s], :] + 0.5 * accs[s][...]
        x1_ref[rows[s], :] = x1
        h2s[s][...] = _rms(x1, nm_ref[...]).astype(BF16)
    for s in range(SUB_TILES):
        g = jnp.dot(h2s[s][...], wg_ref[...], preferred_element_type=F32)
        gt_ref[:, rows[s]] = g.T[0:N_GATE, :]
        for j in range(N_Z_CHUNKS):
            z = jnp.dot(h2s[s][...], wz_ref[:, j * Z_CHUNK:(j + 1) * Z_CHUNK],
                        preferred_element_type=F32)
            for q in range(Z_CHUNK // LANES):
                blk = j * (Z_CHUNK // LANES) + q
                zb = z[:, q * LANES:(q + 1) * LANES]
                if blk < TB_MV:
                    zt_ref[blk, :, rows[s]] = (zb * (M_HEAD_DIM ** -0.5)).T.astype(BF16)
                elif blk < N_T_BLOCKS:
                    zt_ref[blk, :, rows[s]] = zb.T.astype(BF16)
                else:
                    z_ref[blk - N_T_BLOCKS, rows[s], :] = zb.astype(BF16)


def _front_call(x, n1, w1, w3, w2, nm, wz, wg):
    n = x.shape[0]
    tm = TOKEN_TILE

    return pl.pallas_call(
        _front_kernel,
        grid=(n // tm,),
        in_specs=[
            pl.BlockSpec((tm, D_MODEL), lambda i: (i, 0)),
            _const_spec((1, D_MODEL)),
            _const_spec((D_MODEL, D_FF)),
            _const_spec((D_MODEL, D_FF)),
            _const_spec((D_FF, D_MODEL)),
            _const_spec((1, D_MODEL)),
            _const_spec((D_MODEL, Z_COLS)),
            _const_spec((D_MODEL, LANES)),
        ],
        out_specs=[
            pl.BlockSpec((tm, D_MODEL), lambda i: (i, 0)),
            pl.BlockSpec((N_Z_BLOCKS, tm, LANES), lambda i: (0, i, 0)),
            pl.BlockSpec((N_T_BLOCKS, LANES, tm), lambda i: (0, 0, i)),
            pl.BlockSpec((N_GATE, tm), lambda i: (0, i)),
        ],
        out_shape=[
            jax.ShapeDtypeStruct((n, D_MODEL), F32),
            jax.ShapeDtypeStruct((N_Z_BLOCKS, n, LANES), BF16),
            jax.ShapeDtypeStruct((N_T_BLOCKS, LANES, n), BF16),
            jax.ShapeDtypeStruct((N_GATE, n), F32),
        ],
        scratch_shapes=(
            [pltpu.VMEM((tm // SUB_TILES, D_MODEL), BF16)] * (2 * SUB_TILES)
            + [pltpu.VMEM((tm // SUB_TILES, D_MODEL), F32)] * SUB_TILES),
        compiler_params=pltpu.CompilerParams(
            dimension_semantics=("arbitrary",), vmem_limit_bytes=VMEM_LIMIT_BYTES),
        name="front_ffn_proj",
    )(x, n1, w1, w3, w2, nm, wz, wg)


def _back_kernel(x1_ref, gm_ref, ga_ref, hm_ref, ha_ref, wpm_ref, wpa_ref, wo_ref,
                 n2_ref, w1_ref, w3_ref, w2_ref, y_ref, *scratch):
    hs, h2s, accs, x2s = (scratch[k * SUB_TILES:(k + 1) * SUB_TILES] for k in range(4))
    sub = x1_ref.shape[0] // SUB_TILES
    rows = [slice(s * sub, (s + 1) * sub) for s in range(SUB_TILES)]

    pms = [lax.dot_general(hm_ref[:, rows[s]], wpm_ref[...], (((0,), (0,)), ((), ())),
                           preferred_element_type=F32) for s in range(SUB_TILES)]
    pas = [jnp.dot(ha_ref[rows[s], :], wpa_ref[...], preferred_element_type=F32)
           for s in range(SUB_TILES)]
    for s in range(SUB_TILES):
        for j in range(D_MODEL // LANES):
            sl = slice(j * LANES, (j + 1) * LANES)
            gm = jax.nn.sigmoid(gm_ref[j, rows[s], :].astype(F32))
            ga = jax.nn.sigmoid(ga_ref[j, rows[s], :].astype(F32))
            hs[s][:, sl] = (gm * pms[s][:, sl] + ga * pas[s][:, sl]).astype(BF16)
    for s in range(SUB_TILES):
        x2s[s][...] = x1_ref[rows[s], :] + jnp.dot(hs[s][...], wo_ref[...],
                                                   preferred_element_type=F32)
    for s in range(SUB_TILES):
        h2s[s][...] = _rms(x2s[s][...], n2_ref[...]).astype(BF16)
    for s in range(SUB_TILES):
        _swiglu_into(h2s[s], w1_ref, w3_ref, w2_ref, accs[s])
    for s in range(SUB_TILES):
        y_ref[rows[s], :] = x2s[s][...] + 0.5 * accs[s][...]


def _back_call(x1, z, hm, ha, wpm, wpa, wo, n2, w1, w3, w2):
    n = x1.shape[0]
    tm = TOKEN_TILE
    gblk = D_MODEL // LANES
    return pl.pallas_call(
        _back_kernel,
        grid=(n // tm,),
        in_specs=[
            pl.BlockSpec((tm, D_MODEL), lambda i: (i, 0)),
            pl.BlockSpec((gblk, tm, LANES), lambda i: (ZB_GM // gblk, i, 0)),
            pl.BlockSpec((gblk, tm, LANES), lambda i: (ZB_GA // gblk, i, 0)),
            pl.BlockSpec((M_WIDTH, tm), lambda i: (0, i)),
            pl.BlockSpec((tm, A_WIDTH), lambda i: (i, 0)),
            _const_spec((M_WIDTH, D_MODEL)),
            _const_spec((A_WIDTH, D_MODEL)),
            _const_spec((D_MODEL, D_MODEL)),
            _const_spec((1, D_MODEL)),
            _const_spec((D_MODEL, D_FF)),
            _const_spec((D_MODEL, D_FF)),
            _const_spec((D_FF, D_MODEL)),
        ],
        out_specs=pl.BlockSpec((tm, D_MODEL), lambda i: (i, 0)),
        out_shape=jax.ShapeDtypeStruct((n, D_MODEL), F32),
        scratch_shapes=(
            [pltpu.VMEM((tm // SUB_TILES, D_MODEL), BF16)] * (2 * SUB_TILES)
            + [pltpu.VMEM((tm // SUB_TILES, D_MODEL), F32)] * (2 * SUB_TILES)),
        compiler_params=pltpu.CompilerParams(
            dimension_semantics=("arbitrary",), vmem_limit_bytes=VMEM_LIMIT_BYTES),
        name="back_merge_ffn",
    )(x1, z, z, hm, ha, wpm, wpa, wo, n2, w1, w3, w2)


def _split3(x):
    h1 = x.astype(BF16)
    r1 = x - h1.astype(F32)
    h2 = r1.astype(BF16)
    h3 = (r1 - h2.astype(F32)).astype(BF16)
    return h1, h2, h3


def _mlstm_kernel(qt_ref, k_ref, vt_ref, ot_ref, g_ref, bg_ref, mn_ref, out_ref,
                  gi_ref, lf_ref, pf_ref, sf_ref, w_ref, mp_ref, mq_ref,
                  so_ref, sn_ref, dcn_ref, cf_ref, cb_ref, st_ref, *, seq):
    L = M_CHUNK
    nc = seq // L
    hd = pl.program_id(1)
    row = lax.broadcasted_iota(jnp.int32, (L, L), 0)
    col = lax.broadcasted_iota(jnp.int32, (L, L), 1)
    lower = col <= row
    upper = col >= row
    diag = col == row
    dirs = ((upper, cf_ref, pf_ref, 0, 4, L - 1, False),
            (lower, cb_ref, sf_ref, 8, 12, 0, True))

    @pl.when(hd == 0)
    def _():
        gt = g_ref[...].reshape(N_GATE * nc, LANES) + bg_ref[...]
        gi_ref[...] = gt * LOG2E
        lf_ref[...] = (jnp.minimum(gt, 0.0) - jnp.log1p(jnp.exp(-jnp.abs(gt)))) * LOG2E
        h1, h2, h3 = _split3(lf_ref[...])
        incl_prefix = upper.astype(BF16)
        incl_suffix = lower.astype(BF16)
        pf_ref[...] = (jnp.dot(h1, incl_prefix, preferred_element_type=F32)
                       + jnp.dot(h2, incl_prefix, preferred_element_type=F32)
                       + jnp.dot(h3, incl_prefix, preferred_element_type=F32))
        sf_ref[...] = (jnp.dot(h1, incl_suffix, preferred_element_type=F32)
                       + jnp.dot(h2, incl_suffix, preferred_element_type=F32)
                       + jnp.dot(h3, incl_suffix, preferred_element_type=F32))

    ones_rows = jnp.ones((ONES_ROWS, L), BF16)

    def v_aug(cs):
        return jnp.concatenate([vt_ref[0, :, cs], ones_rows], axis=0)

    for d, (_, _, cum_ref, i_off, f_off, g_lane, rev) in enumerate(dirs):
        li = gi_ref[pl.ds(pl.multiple_of((i_off + hd) * nc, 8), nc), :]
        bc = cum_ref[pl.ds(pl.multiple_of((f_off + hd) * nc, 8), nc), :]
        g = jnp.broadcast_to(bc[:, g_lane:g_lane + 1], (nc, LANES))
        a = g + (li - bc)
        m_loc = jnp.broadcast_to(jnp.max(a, axis=1, keepdims=True), (nc, LANES))
        w_ref[d] = jnp.exp2(a - m_loc)
        m = jnp.full((1, LANES), NEG, F32)
        for c in (range(nc - 1, -1, -1) if rev else range(nc)):
            mp_ref[d, c:c + 1, :] = m
            m = jnp.maximum(g[c:c + 1, :] + m, m_loc[c:c + 1, :])
            mq_ref[d, c:c + 1, :] = m
        so_ref[d] = jnp.exp2(g + mp_ref[d] - mq_ref[d])
        sn_ref[d] = jnp.exp2(m_loc - mq_ref[d])

    def inc_group(i, carry):
        chunks = [i * OUT_GROUP + g for g in range(OUT_GROUP)]
        spans = [pl.ds(pl.multiple_of(c * L, L), L) for c in chunks]
        lhs = []
        for c, cs in zip(chunks, spans):
            vat = v_aug(cs).astype(F32)
            lhs.append(jnp.concatenate([vat * w_ref[0, pl.ds(c, 1), :],
                                        vat * w_ref[1, pl.ds(c, 1), :]], axis=0).astype(BF16))
        for c, cs, lh in zip(chunks, spans, lhs):
            dcn_ref[c] = jnp.dot(lh, k_ref[0, cs, :], preferred_element_type=F32)
        return carry

    lax.fori_loop(0, nc // OUT_GROUP, inc_group, 0)

    st_ref[...] = jnp.zeros_like(st_ref)

    def scan_step(i, carry):
        for d, (_, c_ref, _, _, _, _, rev) in enumerate(dirs):
            c = (nc - 1 - i) if rev else i
            st = st_ref[d]
            c_ref[c] = st.astype(BF16)
            st_ref[d] = (so_ref[d, pl.ds(c, 1), :] * st
                         + sn_ref[d, pl.ds(c, 1), :] * dcn_ref[c, d * VA_ROWS:(d + 1) * VA_ROWS, :])
        return carry

    lax.fori_loop(0, nc, scan_step, 0)

    def out_group(i, carry):
        chunks = [i * OUT_GROUP + g for g in range(OUT_GROUP)]
        spans = [pl.ds(pl.multiple_of(c * L, L), L) for c in chunks]
        qtb = [qt_ref[0, :, cs] for cs in spans]
        smt = [jnp.dot(k_ref[0, cs, :], q, preferred_element_type=F32)
               for cs, q in zip(spans, qtb)]
        rhs, floor = [], []
        for c, q, sm in zip(chunks, qtb, smt):
            for d, (mask, _, cum_ref, i_off, f_off, _, _) in enumerate(dirs):
                cum = cum_ref[pl.ds((f_off + hd) * nc + c, 1), :]
                r = gi_ref[pl.ds((i_off + hd) * nc + c, 1), :] - cum
                mp = mp_ref[d, pl.ds(c, 1), :]
                r_col = jnp.sum(jnp.where(diag, r, 0.0), axis=1, keepdims=True)
                rm = jnp.where(mask, r_col, NEG)
                cm = jnp.maximum(jnp.max(rm, axis=0, keepdims=True), mp)
                wq = (jnp.exp2(rm - cm) * sm).astype(BF16)
                sq = q * jnp.exp2(mp - cm).astype(BF16)
                rhs.append(jnp.concatenate([wq, sq], axis=0))
                floor.append(jnp.exp2(-(cum + cm)))
        ys = []
        for g, (c, cs) in enumerate(zip(chunks, spans)):
            vat = v_aug(cs)
            for d, (_, c_ref, _, _, _, _, _) in enumerate(dirs):
                ys.append(jnp.dot(jnp.concatenate([vat, c_ref[c]], axis=1), rhs[2 * g + d],
                                  preferred_element_type=F32))
        for g, cs in enumerate(spans):
            hsum = jnp.zeros((M_HEAD_DIM, L), F32)
            for d in range(2):
                y = ys[2 * g + d]
                den = jnp.maximum(jnp.abs(y[M_HEAD_DIM:M_HEAD_DIM + 1, :]), floor[2 * g + d])
                hsum = hsum + y[0:M_HEAD_DIM, :] * (1.0 / den)
            ms = jnp.sum(hsum * hsum, axis=0, keepdims=True) * (1.0 / M_HEAD_DIM)
            hn = hsum * lax.rsqrt(ms + EPS)
            gate = jax.nn.sigmoid(ot_ref[0, :, cs].astype(F32))
            out_ref[:, cs] = (hn * mn_ref[...] * gate).astype(BF16)
        return carry

    lax.fori_loop(0, nc // OUT_GROUP, out_group, 0)


def _mlstm_call(z, zt, g, bg, mn, batch, seq):
    n = batch * seq
    nc = seq // M_CHUNK
    dk = M_HEAD_DIM

    def tspec(base):
        return pl.BlockSpec((1, LANES, seq), lambda b, h: (base + h, 0, b))

    return pl.pallas_call(
        functools.partial(_mlstm_kernel, seq=seq),
        grid=(batch, M_HEADS),
        in_specs=[
            tspec(TB_MQ),
            pl.BlockSpec((1, seq, LANES), lambda b, h: (ZB_MK + h, b, 0)),
            tspec(TB_MV), tspec(TB_MO),
            pl.BlockSpec((N_GATE, 1, nc, LANES), lambda b, h: (0, b, 0, 0)),
            pl.BlockSpec((N_GATE * nc, LANES), lambda b, h: (0, 0)),
            pl.BlockSpec((dk, LANES), lambda b, h: (h, 0)),
        ],
        out_specs=pl.BlockSpec((dk, seq), lambda b, h: (h, b)),
        out_shape=jax.ShapeDtypeStruct((M_WIDTH, n), BF16),
        scratch_shapes=[
            pltpu.VMEM((nc * N_GATE, LANES), F32),
            pltpu.VMEM((nc * N_GATE, LANES), F32),
            pltpu.VMEM((nc * N_GATE, LANES), F32),
            pltpu.VMEM((nc * N_GATE, LANES), F32),
            pltpu.VMEM((2, nc, LANES), F32),
            pltpu.VMEM((2, nc, LANES), F32),
            pltpu.VMEM((2, nc, LANES), F32),
            pltpu.VMEM((2, nc, LANES), F32),
            pltpu.VMEM((2, nc, LANES), F32),
            pltpu.VMEM((nc, 2 * VA_ROWS, dk), F32),
            pltpu.VMEM((nc, VA_ROWS, dk), BF16),
            pltpu.VMEM((nc, VA_ROWS, dk), BF16),
            pltpu.VMEM((2, VA_ROWS, dk), F32),
        ],
        compiler_params=pltpu.CompilerParams(
            dimension_semantics=("arbitrary", "arbitrary"), vmem_limit_bytes=VMEM_LIMIT_BYTES),
        name="mlstm",
    )(zt, z, zt, zt, g, bg, mn)


def _attn_kernel(sink_ref, q_ref, k_ref, v_ref, cos_ref, sin_ref, qn_ref, kn_ref, out_ref,
                 qs_ref, ks_ref, vs_ref, *, seq):
    W = WINDOW
    nb = seq // W
    kv = pl.program_id(1)
    lane = lax.broadcasted_iota(jnp.int32, (1, LANES), 1)
    first_half = (lane % A_HEAD_DIM) < (A_HEAD_DIM // 2)
    head_lo = lane < A_HEAD_DIM
    gr = lax.broadcasted_iota(jnp.int32, (LANES, LANES), 0) // A_HEAD_DIM
    gc = lax.broadcasted_iota(jnp.int32, (LANES, LANES), 1) // A_HEAD_DIM
    group_mean = jnp.where(gr == gc, 1.0 / A_HEAD_DIM, 0.0).astype(BF16)
    rows_blk = 256

    half = A_HEAD_DIM // 2
    pr = lax.broadcasted_iota(jnp.int32, (LANES, LANES), 0)
    pc = lax.broadcasted_iota(jnp.int32, (LANES, LANES), 1)
    pc_first = (pc % A_HEAD_DIM) < half
    rot_mat = jnp.where(pc_first & (pr == pc + half), -1.0,
                        jnp.where((~pc_first) & (pr == pc - half), 1.0, 0.0)).astype(BF16)

    def rot_gain(g_ref):
        g = jnp.broadcast_to(g_ref[...], (8, LANES))
        return jnp.where(first_half, pltpu.roll(g, LANES - half, 1), pltpu.roll(g, half, 1))[0:1, :]

    q_scale = (A_HEAD_DIM ** -0.5) * LOG2E
    qg, qg_rot = qn_ref[...] * q_scale, rot_gain(qn_ref) * q_scale
    kg, kg_rot = kn_ref[...], rot_gain(kn_ref)

    def norm_rope(xb, ct, st):
        x = xb.astype(F32)
        sq = x * x
        hi = sq.astype(BF16)
        lo = (sq - hi.astype(F32)).astype(BF16)
        ms = (jnp.dot(hi, group_mean, preferred_element_type=F32)
              + jnp.dot(lo, group_mean, preferred_element_type=F32))
        rx = jnp.dot(xb, rot_mat, preferred_element_type=F32)
        return ((x * ct + rx * st) * lax.rsqrt(ms + EPS)).astype(BF16)

    def prep(i, carry):
        rs = pl.ds(pl.multiple_of(i * rows_blk, rows_blk), rows_blk)
        cs = cos_ref[rs, :]
        sn = sin_ref[rs, :]
        qct, qst = cs * qg, sn * qg_rot
        for j in range(2):
            qs_ref[j, rs, :] = norm_rope(q_ref[j, rs, :], qct, qst)
        ks_ref[pl.ds(pl.multiple_of(W + i * rows_blk, W), rows_blk), :] = norm_rope(
            k_ref[0, rs, :], cs * kg, sn * kg_rot)
        return carry

    lax.fori_loop(0, seq // rows_blk, prep, 0)
    zpad = jnp.zeros((W, LANES), BF16)
    ks_ref[0:W, :] = zpad
    ks_ref[seq + W:seq + 2 * W, :] = zpad
    vs_ref[0:W, :] = zpad
    vs_ref[seq + W:seq + 2 * W, :] = zpad
    vs_ref[W:seq + W, :] = v_ref[0]

    qi = lax.broadcasted_iota(jnp.int32, (W, W), 0)
    kj = lax.broadcasted_iota(jnp.int32, (W, W), 1)
    left_band = jnp.where(kj >= qi, 0.0, NEG)
    right_band = jnp.where(kj <= qi, 0.0, NEG)
    zeros_mid = jnp.zeros((W, W), F32)

    def blk_group(i, carry):
        blocks = [i * ATTN_GROUP + j for j in range(ATTN_GROUP)]
        qrows = [pl.ds(pl.multiple_of(n * W, W), W) for n in blocks]
        wins = [pl.ds(pl.multiple_of(n * W, W), 3 * W) for n in blocks]
        scores = []
        for qr, ws in zip(qrows, wins):
            q0 = qs_ref[0, qr, :]
            q1 = qs_ref[1, qr, :]
            zq = jnp.zeros_like(q0)
            qstack = jnp.concatenate([
                jnp.where(head_lo, q0, zq), jnp.where(head_lo, zq, q0),
                jnp.where(head_lo, q1, zq), jnp.where(head_lo, zq, q1)], axis=0)
            scores.append(lax.dot_general(qstack, ks_ref[ws, :], (((1,), (1,)), ((), ())),
                                          preferred_element_type=F32))
        probs, dens = [], []
        for n, s in zip(blocks, scores):
            bias_l = left_band + jnp.where(n == 0, NEG, 0.0)
            bias_r = right_band + jnp.where(n == nb - 1, NEG, 0.0)
            for hh in range(A_GROUP):
                rows = slice(hh * W, (hh + 1) * W)
                sh = jnp.concatenate([s[rows, 0:W] + bias_l, s[rows, W:2 * W],
                                      s[rows, 2 * W:3 * W] + bias_r], axis=1)
                sk = sink_ref[kv * A_GROUP + hh] * LOG2E
                m = jnp.maximum(jnp.max(sh, axis=1, keepdims=True), sk)
                p = jnp.exp2(sh - m)
                dens.append(jnp.sum(p, axis=1, keepdims=True) + jnp.exp2(sk - m))
                probs.append(p.astype(BF16))
        pvs = [jnp.dot(p, vs_ref[wins[j // A_GROUP], :], preferred_element_type=F32)
               for j, p in enumerate(probs)]
        for b, qr in enumerate(qrows):
            o = [pvs[b * A_GROUP + hh] / dens[b * A_GROUP + hh] for hh in range(A_GROUP)]
            out_ref[qr, 0:LANES] = jnp.where(head_lo, o[0], o[1]).astype(BF16)
            out_ref[qr, LANES:2 * LANES] = jnp.where(head_lo, o[2], o[3]).astype(BF16)
        return carry

    lax.fori_loop(0, nb // ATTN_GROUP, blk_group, 0)


def _attn_call(z, sink, cos_t, sin_t, qn, kn, batch, seq):
    n = batch * seq
    return pl.pallas_call(
        functools.partial(_attn_kernel, seq=seq),
        grid=(batch, A_KV_HEADS),
        in_specs=[
            pl.BlockSpec(memory_space=pltpu.SMEM),
            pl.BlockSpec((2, seq, LANES), lambda b, g: (ZB_AQ // 2 + g, b, 0)),
            pl.BlockSpec((1, seq, LANES), lambda b, g: (ZB_AK + g, b, 0)),
            pl.BlockSpec((1, seq, LANES), lambda b, g: (ZB_AV + g, b, 0)),
            pl.BlockSpec((seq, LANES), lambda b, g: (0, 0)),
            pl.BlockSpec((seq, LANES), lambda b, g: (0, 0)),
            pl.BlockSpec((1, LANES), lambda b, g: (0, 0)),
            pl.BlockSpec((1, LANES), lambda b, g: (0, 0)),
        ],
        out_specs=pl.BlockSpec((seq, 2 * LANES), lambda b, g: (b, g)),
        out_shape=jax.ShapeDtypeStruct((n, A_WIDTH), BF16),
        scratch_shapes=[
            pltpu.VMEM((2, seq, LANES), BF16),
            pltpu.VMEM((seq + 2 * WINDOW, LANES), BF16),
            pltpu.VMEM((seq + 2 * WINDOW, LANES), BF16),
        ],
        compiler_params=pltpu.CompilerParams(
            dimension_semantics=("arbitrary", "arbitrary"), vmem_limit_bytes=VMEM_LIMIT_BYTES),
        name="window_attn",
    )(sink, z, z, z, cos_t, sin_t, qn, kn)


def _z_weights(w):
    mq, mk, mv, mo, aq, ak, av, gm = 0, 512, 1024, 1536, 2064, 2576, 2704, 2832
    parts = [(mq, M_WIDTH), (mv, M_WIDTH), (mo, M_WIDTH), (gm, 2 * D_MODEL), (mk, M_WIDTH), (aq, A_WIDTH)]
    for base in (ak, av):
        for g in range(A_KV_HEADS):
            parts += [(base + g * A_HEAD_DIM, A_HEAD_DIM)] * 2
    return jnp.concatenate([w[:, a:a + n] for a, n in parts], axis=1).astype(BF16)


def _ffn_weights(w1, w3, w2):
    return w1.astype(BF16), w3.astype(BF16), w2.astype(BF16)


def _rope_tables(seq):
    half = A_HEAD_DIM // 2
    inv = jnp.power(ROPE_THETA, -jnp.arange(half, dtype=F32) / half)
    ang = jnp.arange(seq).astype(F32)[:, None] * inv[None, :]
    return jnp.tile(jnp.cos(ang), (1, LANES // half)), jnp.tile(jnp.sin(ang), (1, LANES // half))


def _layer(x, p):
    batch, seq, _ = x.shape
    xf = x.reshape(batch * seq, D_MODEL)
    nc = seq // M_CHUNK
    x1, z, zt, gt = _front_call(xf, p["n1"], *p["ffn1"], p["nm"], p["wz"], p["wg"])
    hm = _mlstm_call(z, zt, gt.reshape(N_GATE, batch, nc, M_CHUNK),
                     jnp.broadcast_to(jnp.repeat(p["bg"], nc)[:, None], (N_GATE * nc, LANES)),
                     p["mn"], batch, seq)
    cos_t, sin_t = _rope_tables(seq)
    ha = _attn_call(z, p["sink"], cos_t, sin_t, p["qn"], p["kn"], batch, seq)
    y = _back_call(x1, z, hm, ha, p["wpm"], p["wpa"], p["wo"], p["n2"], *p["ffn2"])
    return y.reshape(batch, seq, D_MODEL)


def kernel(x_prompt, x_sample, ffn1_norm, ffn1_w1, ffn1_w3, ffn1_w2, mix_norm, w_in, b_gates, m_norm, q_norm, k_norm, sink, w_pm, w_pa, w_out, ffn2_norm, ffn2_w1, ffn2_w3, ffn2_w2):
    depth = w_in.shape[0]
    y_prompt, y_sample = x_prompt, x_sample
    for l in range(depth):
        p = {
            "n1": ffn1_norm[l][None, :],
            "ffn1": _ffn_weights(ffn1_w1[l], ffn1_w3[l], ffn1_w2[l]),
            "nm": mix_norm[l][None, :],
            "wz": _z_weights(w_in[l]),
            "wg": jnp.pad(w_in[l][:, 2048:2048 + N_GATE], ((0, 0), (0, LANES - N_GATE))).astype(BF16),
            "bg": b_gates[l],
            "mn": jnp.broadcast_to(m_norm[l][:, None], (M_WIDTH, LANES)),
            "qn": jnp.tile(q_norm[l], LANES // A_HEAD_DIM)[None, :],
            "kn": jnp.tile(k_norm[l], LANES // A_HEAD_DIM)[None, :],
            "sink": sink[l],
            "wpm": w_pm[l].astype(BF16),
            "wpa": w_pa[l].astype(BF16),
            "wo": w_out[l].astype(BF16),
            "n2": ffn2_norm[l][None, :],
            "ffn2": _ffn_weights(ffn2_w1[l], ffn2_w3[l], ffn2_w2[l]),
        }
        y_prompt = _layer(y_prompt, p)
        y_sample = _layer(y_sample, p)
    return (y_prompt, y_sample)
```

```python
import functools

import numpy as np
import jax
import jax.numpy as jnp
from jax import lax
from jax.experimental import pallas as pl
from jax.experimental.pallas import tpu as pltpu

F32 = jnp.float32
BF16 = jnp.bfloat16

D_MODEL = 1024
D_FF = 2816
M_HEADS = 4
M_HEAD_DIM = 128
M_WIDTH = 512
N_GATE = 16
A_HEADS = 8
A_KV_HEADS = 2
A_GROUP = 4
A_HEAD_DIM = 64
A_WIDTH = 512
WINDOW = 128
ROPE_THETA = 10000.0
EPS = 1e-6
NEG = -1e30
LOG2E = 1.4426950408889634

LANES = 128
FF_CHUNK = 256
N_FF_CHUNKS = D_FF // FF_CHUNK
Z_CHUNK = 512
N_T_BLOCKS = 18
N_Z_BLOCKS = 20
Z_COLS = (N_T_BLOCKS + N_Z_BLOCKS + 1) * LANES
TB_MQ, TB_MV, TB_MO, TB_AQ, TB_AK, TB_AV = 0, 4, 8, 12, 16, 17
ZB_GM, ZB_GA, ZB_MK = 0, 8, 16
M_CHUNK = 128
ONES_ROWS = 16
VA_ROWS = M_HEAD_DIM + ONES_ROWS
OUT_GROUP = 8
ATTN_GROUP = 4
TOKEN_TILE = 512
SUB_TILES = 2
VMEM_LIMIT_BYTES = 60 * 1024 * 1024


def _rms(x, g):
    return x * lax.rsqrt(jnp.mean(x * x, axis=-1, keepdims=True) + EPS) * g


def _const_spec(shape):
    nd = len(shape)
    return pl.BlockSpec(shape, lambda *_: (0,) * nd, pipeline_mode=pl.Buffered(1))


def _swiglu_into(h_ref, w1_ref, w3_ref, w2_ref, acc_ref):
    def up(c):
        cols = slice(c * FF_CHUNK, (c + 1) * FF_CHUNK)
        h = h_ref[...]
        return (jnp.dot(h, w1_ref[:, cols], preferred_element_type=F32),
                jnp.dot(h, w3_ref[:, cols], preferred_element_type=F32))

    nxt = up(0)
    for c in range(N_FF_CHUNKS):
        a, b = nxt
        if c + 1 < N_FF_CHUNKS:
            nxt = up(c + 1)
        hid = (a * jax.nn.sigmoid(a) * b).astype(BF16)
        part = jnp.dot(hid, w2_ref[c * FF_CHUNK:(c + 1) * FF_CHUNK, :],
                       preferred_element_type=F32)
        if c == 0:
            acc_ref[...] = part
        else:
            acc_ref[...] += part


def _front_kernel(x_ref, n1_ref, w1_ref, w3_ref, w2_ref, nm_ref, wz_ref,
                  x1_ref, z_ref, zt_ref, gt_ref, *scratch):
    hs, h2s, accs = (scratch[k * SUB_TILES:(k + 1) * SUB_TILES] for k in range(3))
    sub = x_ref.shape[0] // SUB_TILES
    rows = [slice(s * sub, (s + 1) * sub) for s in range(SUB_TILES)]

    for s in range(SUB_TILES):
        hs[s][...] = _rms(x_ref[rows[s], :], n1_ref[...]).astype(BF16)
    for s in range(SUB_TILES):
        _swiglu_into(hs[s], w1_ref, w3_ref, w2_ref, accs[s])
    for s in range(SUB_TILES):
        x1 = x_ref[rows[s], :] + 0.5 * accs[s][...]
        x1_ref[rows[s], :] = x1
        h2s[s][...] = _rms(x1, nm_ref[...]).astype(BF16)
    for s in range(SUB_TILES):
        for c0 in range(0, Z_COLS, Z_CHUNK):
            c1 = min(c0 + Z_CHUNK, Z_COLS)
            z = jnp.dot(h2s[s][...], wz_ref[:, c0:c1], preferred_element_type=F32)
            for q in range((c1 - c0) // LANES):
                blk = c0 // LANES + q
                zb = z[:, q * LANES:(q + 1) * LANES]
                if blk < TB_MV:
                    zt_ref[blk, :, rows[s]] = (zb * (M_HEAD_DIM ** -0.5)).T.astype(BF16)
                elif blk < N_T_BLOCKS:
                    zt_ref[blk, :, rows[s]] = zb.T.astype(BF16)
                elif blk < N_T_BLOCKS + N_Z_BLOCKS:
                    z_ref[blk - N_T_BLOCKS, rows[s], :] = zb.astype(BF16)
                else:
                    gt_ref[:, rows[s]] = zb.T[0:N_GATE, :]


def _front_call(x, n1, w1, w3, w2, nm, wz):
    n = x.shape[0]
    tm = TOKEN_TILE

    return pl.pallas_call(
        _front_kernel,
        grid=(n // tm,),
        in_specs=[
            pl.BlockSpec((tm, D_MODEL), lambda i: (i, 0)),
            _const_spec((1, D_MODEL)),
            _const_spec((D_MODEL, D_FF)),
            _const_spec((D_MODEL, D_FF)),
            _const_spec((D_FF, D_MODEL)),
            _const_spec((1, D_MODEL)),
            _const_spec((D_MODEL, Z_COLS)),
        ],
        out_specs=[
            pl.BlockSpec((tm, D_MODEL), lambda i: (i, 0)),
            pl.BlockSpec((N_Z_BLOCKS, tm, LANES), lambda i: (0, i, 0)),
            pl.BlockSpec((N_T_BLOCKS, LANES, tm), lambda i: (0, 0, i)),
            pl.BlockSpec((N_GATE, tm), lambda i: (0, i)),
        ],
        out_shape=[
            jax.ShapeDtypeStruct((n, D_MODEL), F32),
            jax.ShapeDtypeStruct((N_Z_BLOCKS, n, LANES), BF16),
            jax.ShapeDtypeStruct((N_T_BLOCKS, LANES, n), BF16),
            jax.ShapeDtypeStruct((N_GATE, n), F32),
        ],
        scratch_shapes=(
            [pltpu.VMEM((tm // SUB_TILES, D_MODEL), BF16)] * (2 * SUB_TILES)
            + [pltpu.VMEM((tm // SUB_TILES, D_MODEL), F32)] * SUB_TILES),
        compiler_params=pltpu.CompilerParams(
            dimension_semantics=("arbitrary",), vmem_limit_bytes=VMEM_LIMIT_BYTES),
        name="front_ffn_proj",
    )(x, n1, w1, w3, w2, nm, wz)


def _back_kernel(x1_ref, gm_ref, ga_ref, hm_ref, ha_ref, wpm_ref, wpa_ref, wo_ref,
                 n2_ref, w1_ref, w3_ref, w2_ref, y_ref, *scratch):
    hs, h2s, accs, x2s = (scratch[k * SUB_TILES:(k + 1) * SUB_TILES] for k in range(4))
    sub = x1_ref.shape[0] // SUB_TILES
    rows = [slice(s * sub, (s + 1) * sub) for s in range(SUB_TILES)]

    lead = (((0,), (0,)), ((), ()))
    pms = [lax.dot_general(hm_ref[:, rows[s]], wpm_ref[...], lead, preferred_element_type=F32)
           for s in range(SUB_TILES)]
    pas = [lax.dot_general(ha_ref[:, rows[s]], wpa_ref[...], lead, preferred_element_type=F32)
           for s in range(SUB_TILES)]
    for s in range(SUB_TILES):
        for j in range(D_MODEL // LANES):
            sl = slice(j * LANES, (j + 1) * LANES)
            gm = jax.nn.sigmoid(gm_ref[j, rows[s], :].astype(F32))
            ga = jax.nn.sigmoid(ga_ref[j, rows[s], :].astype(F32))
            hs[s][:, sl] = (gm * pms[s][:, sl] + ga * pas[s][:, sl]).astype(BF16)
    for s in range(SUB_TILES):
        x2s[s][...] = x1_ref[rows[s], :] + jnp.dot(hs[s][...], wo_ref[...],
                                                   preferred_element_type=F32)
    for s in range(SUB_TILES):
        h2s[s][...] = _rms(x2s[s][...], n2_ref[...]).astype(BF16)
    for s in range(SUB_TILES):
        _swiglu_into(h2s[s], w1_ref, w3_ref, w2_ref, accs[s])
    for s in range(SUB_TILES):
        y_ref[rows[s], :] = x2s[s][...] + 0.5 * accs[s][...]


def _back_call(x1, z, hm, ha, wpm, wpa, wo, n2, w1, w3, w2):
    n = x1.shape[0]
    tm = TOKEN_TILE
    gblk = D_MODEL // LANES
    return pl.pallas_call(
        _back_kernel,
        grid=(n // tm,),
        in_specs=[
            pl.BlockSpec((tm, D_MODEL), lambda i: (i, 0)),
            pl.BlockSpec((gblk, tm, LANES), lambda i: (ZB_GM // gblk, i, 0)),
            pl.BlockSpec((gblk, tm, LANES), lambda i: (ZB_GA // gblk, i, 0)),
            pl.BlockSpec((M_WIDTH, tm), lambda i: (0, i)),
            pl.BlockSpec((A_WIDTH, tm), lambda i: (0, i)),
            _const_spec((M_WIDTH, D_MODEL)),
            _const_spec((A_WIDTH, D_MODEL)),
            _const_spec((D_MODEL, D_MODEL)),
            _const_spec((1, D_MODEL)),
            _const_spec((D_MODEL, D_FF)),
            _const_spec((D_MODEL, D_FF)),
            _const_spec((D_FF, D_MODEL)),
        ],
        out_specs=pl.BlockSpec((tm, D_MODEL), lambda i: (i, 0)),
        out_shape=jax.ShapeDtypeStruct((n, D_MODEL), F32),
        scratch_shapes=(
            [pltpu.VMEM((tm // SUB_TILES, D_MODEL), BF16)] * (2 * SUB_TILES)
            + [pltpu.VMEM((tm // SUB_TILES, D_MODEL), F32)] * (2 * SUB_TILES)),
        compiler_params=pltpu.CompilerParams(
            dimension_semantics=("arbitrary",), vmem_limit_bytes=VMEM_LIMIT_BYTES),
        name="back_merge_ffn",
    )(x1, z, z, hm, ha, wpm, wpa, wo, n2, w1, w3, w2)


def _split3(x):
    h1 = x.astype(BF16)
    r1 = x - h1.astype(F32)
    h2 = r1.astype(BF16)
    h3 = (r1 - h2.astype(F32)).astype(BF16)
    return h1, h2, h3


def _mlstm_kernel(qt_ref, k_ref, vt_ref, ot_ref, g_ref, bg_ref, mn_ref, out_ref,
                  gi_ref, lf_ref, pf_ref, sf_ref, w_ref, mp_ref, mq_ref,
                  so_ref, sn_ref, dcn_ref, cf_ref, cb_ref, st_ref, *, seq):
    L = M_CHUNK
    nc = seq // L
    hd = pl.program_id(1)
    row = lax.broadcasted_iota(jnp.int32, (L, L), 0)
    col = lax.broadcasted_iota(jnp.int32, (L, L), 1)
    lower = col <= row
    upper = col >= row
    diag = col == row
    dirs = ((upper, cf_ref, pf_ref, 0, 4, L - 1, False),
            (lower, cb_ref, sf_ref, 8, 12, 0, True))

    @pl.when(hd == 0)
    def _():
        gt = g_ref[...].reshape(N_GATE * nc, LANES) + bg_ref[...]
        gi_ref[...] = gt * LOG2E
        lf_ref[...] = (jnp.minimum(gt, 0.0) - jnp.log1p(jnp.exp(-jnp.abs(gt)))) * LOG2E
        h1, h2, h3 = _split3(lf_ref[...])
        incl_prefix = upper.astype(BF16)
        incl_suffix = lower.astype(BF16)
        pf_ref[...] = (jnp.dot(h1, incl_prefix, preferred_element_type=F32)
                       + jnp.dot(h2, incl_prefix, preferred_element_type=F32)
                       + jnp.dot(h3, incl_prefix, preferred_element_type=F32))
        sf_ref[...] = (jnp.dot(h1, incl_suffix, preferred_element_type=F32)
                       + jnp.dot(h2, incl_suffix, preferred_element_type=F32)
                       + jnp.dot(h3, incl_suffix, preferred_element_type=F32))

    ones_rows = jnp.ones((ONES_ROWS, L), BF16)

    def v_aug(cs):
        return jnp.concatenate([vt_ref[0, :, cs], ones_rows], axis=0)

    for d, (_, _, cum_ref, i_off, f_off, g_lane, rev) in enumerate(dirs):
        li = gi_ref[pl.ds(pl.multiple_of((i_off + hd) * nc, 8), nc), :]
        bc = cum_ref[pl.ds(pl.multiple_of((f_off + hd) * nc, 8), nc), :]
        g = jnp.broadcast_to(bc[:, g_lane:g_lane + 1], (nc, LANES))
        a = g + (li - bc)
        m_loc = jnp.broadcast_to(jnp.max(a, axis=1, keepdims=True), (nc, LANES))
        w_ref[d] = jnp.exp2(a - m_loc)
        m = jnp.full((1, LANES), NEG, F32)
        for c in (range(nc - 1, -1, -1) if rev else range(nc)):
            mp_ref[d, c:c + 1, :] = m
            m = jnp.maximum(g[c:c + 1, :] + m, m_loc[c:c + 1, :])
            mq_ref[d, c:c + 1, :] = m
        so_ref[d] = jnp.exp2(g + mp_ref[d] - mq_ref[d])
        sn_ref[d] = jnp.exp2(m_loc - mq_ref[d])

    def inc_group(i, carry):
        chunks = [i * OUT_GROUP + g for g in range(OUT_GROUP)]
        spans = [pl.ds(pl.multiple_of(c * L, L), L) for c in chunks]
        lhs = []
        for c, cs in zip(chunks, spans):
            vat = v_aug(cs).astype(F32)
            lhs.append(jnp.concatenate([vat * w_ref[0, pl.ds(c, 1), :],
                                        vat * w_ref[1, pl.ds(c, 1), :]], axis=0).astype(BF16))
        for c, cs, lh in zip(chunks, spans, lhs):
            dcn_ref[c] = jnp.dot(lh, k_ref[0, cs, :], preferred_element_type=F32)
        return carry

    lax.fori_loop(0, nc // OUT_GROUP, inc_group, 0)

    st_ref[...] = jnp.zeros_like(st_ref)

    def scan_step(i, carry):
        for d, (_, c_ref, _, _, _, _, rev) in enumerate(dirs):
            c = (nc - 1 - i) if rev else i
            st = st_ref[d]
            c_ref[c] = st.astype(BF16)
            st_ref[d] = (so_ref[d, pl.ds(c, 1), :] * st
                         + sn_ref[d, pl.ds(c, 1), :] * dcn_ref[c, d * VA_ROWS:(d + 1) * VA_ROWS, :])
        return carry

    lax.fori_loop(0, nc, scan_step, 0)

    def out_group(i, carry):
        chunks = [i * OUT_GROUP + g for g in range(OUT_GROUP)]
        spans = [pl.ds(pl.multiple_of(c * L, L), L) for c in chunks]
        qtb = [qt_ref[0, :, cs] for cs in spans]
        smt = [jnp.dot(k_ref[0, cs, :], q, preferred_element_type=F32)
               for cs, q in zip(spans, qtb)]
        rhs, floor = [], []
        for c, q, sm in zip(chunks, qtb, smt):
            for d, (mask, _, cum_ref, i_off, f_off, _, _) in enumerate(dirs):
                cum = cum_ref[pl.ds((f_off + hd) * nc + c, 1), :]
                r = gi_ref[pl.ds((i_off + hd) * nc + c, 1), :] - cum
                mp = mp_ref[d, pl.ds(c, 1), :]
                r_col = jnp.sum(jnp.where(diag, r, 0.0), axis=1, keepdims=True)
                rm = jnp.where(mask, r_col, NEG)
                cm = jnp.maximum(jnp.max(rm, axis=0, keepdims=True), mp)
                wq = (jnp.exp2(rm - cm) * sm).astype(BF16)
                sq = q * jnp.exp2(mp - cm).astype(BF16)
                rhs.append(jnp.concatenate([wq, sq], axis=0))
                floor.append(jnp.exp2(-(cum + cm)))
        ys = []
        for g, (c, cs) in enumerate(zip(chunks, spans)):
            vat = v_aug(cs)
            for d, (_, c_ref, _, _, _, _, _) in enumerate(dirs):
                ys.append(jnp.dot(jnp.concatenate([vat, c_ref[c]], axis=1), rhs[2 * g + d],
                                  preferred_element_type=F32))
        for g, cs in enumerate(spans):
            hsum = jnp.zeros((M_HEAD_DIM, L), F32)
            for d in range(2):
                y = ys[2 * g + d]
                den = jnp.maximum(jnp.abs(y[M_HEAD_DIM:M_HEAD_DIM + 1, :]), floor[2 * g + d])
                hsum = hsum + y[0:M_HEAD_DIM, :] * (1.0 / den)
            ms = jnp.sum(hsum * hsum, axis=0, keepdims=True) * (1.0 / M_HEAD_DIM)
            hn = hsum * lax.rsqrt(ms + EPS)
            gate = jax.nn.sigmoid(ot_ref[0, :, cs].astype(F32))
            out_ref[:, cs] = (hn * mn_ref[...] * gate).astype(BF16)
        return carry

    lax.fori_loop(0, nc // OUT_GROUP, out_group, 0)


def _mlstm_call(z, zt, g, bg, mn, batch, seq):
    n = batch * seq
    nc = seq // M_CHUNK
    dk = M_HEAD_DIM

    def tspec(base):
        return pl.BlockSpec((1, LANES, seq), lambda b, h: (base + h, 0, b))

    return pl.pallas_call(
        functools.partial(_mlstm_kernel, seq=seq),
        grid=(batch, M_HEADS),
        in_specs=[
            tspec(TB_MQ),
            pl.BlockSpec((1, seq, LANES), lambda b, h: (ZB_MK + h, b, 0)),
            tspec(TB_MV), tspec(TB_MO),
            pl.BlockSpec((N_GATE, 1, nc, LANES), lambda b, h: (0, b, 0, 0)),
            pl.BlockSpec((N_GATE * nc, LANES), lambda b, h: (0, 0)),
            pl.BlockSpec((dk, LANES), lambda b, h: (h, 0)),
        ],
        out_specs=pl.BlockSpec((dk, seq), lambda b, h: (h, b)),
        out_shape=jax.ShapeDtypeStruct((M_WIDTH, n), BF16),
        scratch_shapes=[
            pltpu.VMEM((nc * N_GATE, LANES), F32),
            pltpu.VMEM((nc * N_GATE, LANES), F32),
            pltpu.VMEM((nc * N_GATE, LANES), F32),
            pltpu.VMEM((nc * N_GATE, LANES), F32),
            pltpu.VMEM((2, nc, LANES), F32),
            pltpu.VMEM((2, nc, LANES), F32),
            pltpu.VMEM((2, nc, LANES), F32),
            pltpu.VMEM((2, nc, LANES), F32),
            pltpu.VMEM((2, nc, LANES), F32),
            pltpu.VMEM((nc, 2 * VA_ROWS, dk), F32),
            pltpu.VMEM((nc, VA_ROWS, dk), BF16),
            pltpu.VMEM((nc, VA_ROWS, dk), BF16),
            pltpu.VMEM((2, VA_ROWS, dk), F32),
        ],
        compiler_params=pltpu.CompilerParams(
            dimension_semantics=("arbitrary", "arbitrary"), vmem_limit_bytes=VMEM_LIMIT_BYTES),
        name="mlstm",
    )(zt, z, zt, zt, g, bg, mn)


def _attn_kernel(sink_ref, qt_ref, kt_ref, vt_ref, cos_ref, sin_ref, qg_ref, kg_ref, out_ref,
                 qs_ref, ks_ref, vs_ref, *, seq):
    W = WINDOW
    nb = seq // W
    kv = pl.program_id(1)
    dh = A_HEAD_DIM
    half = dh // 2

    def swap_halves(a):
        return jnp.concatenate([a[half:], a[:half]], axis=0)

    def rot_half(a):
        return jnp.concatenate([-a[half:], a[:half]], axis=0)

    q_scale = (dh ** -0.5) * LOG2E
    qg, kg = qg_ref[...] * q_scale, kg_ref[...]
    qg_sw, kg_sw = swap_halves(qg), swap_halves(kg)

    def norm_rope(x, ct, st):
        rinv = lax.rsqrt(jnp.sum(x * x, axis=0, keepdims=True) * (1.0 / dh) + EPS)
        return (x * ct + rot_half(x) * st) * rinv

    def prep(i, carry):
        ls = pl.ds(pl.multiple_of(i * W, W), W)
        cs = cos_ref[:, ls]
        sn = sin_ref[:, ls]
        qct, qst = cs * qg, sn * qg_sw
        for j in range(2):
            xq = qt_ref[j, :, ls].astype(F32)
            for hh in range(2):
                r0 = (2 * j + hh) * dh
                qs_ref[r0:r0 + dh, ls] = norm_rope(xq[hh * dh:(hh + 1) * dh], qct, qst).astype(BF16)
        xk = kt_ref[0, :, ls].astype(F32)
        kct, kst = cs * kg, sn * kg_sw
        k2 = jnp.concatenate([norm_rope(xk[g * dh:(g + 1) * dh], kct, kst)
                              for g in range(A_KV_HEADS)], axis=0)
        ks_ref[pl.ds(pl.multiple_of(W + i * W, W), W), :] = k2.T.astype(BF16)
        return carry

    lax.fori_loop(0, nb, prep, 0, unroll=2)
    ks_ref[0:W, :] = jnp.zeros((W, LANES), BF16)
    ks_ref[seq + W:seq + 2 * W, :] = jnp.zeros((W, LANES), BF16)
    vs_ref[:, 0:W] = jnp.zeros((LANES, W), BF16)
    vs_ref[:, seq + W:seq + 2 * W] = jnp.zeros((LANES, W), BF16)
    vs_ref[:, W:seq + W] = vt_ref[0]

    nq = A_GROUP * W
    kj = lax.broadcasted_iota(jnp.int32, (W, nq), 0)
    qi = lax.broadcasted_iota(jnp.int32, (W, nq), 1) % W
    left_band = jnp.where(kj >= qi, 0.0, NEG)
    right_band = jnp.where(kj <= qi, 0.0, NEG)
    head_of_lane = lax.broadcasted_iota(jnp.int32, (1, nq), 1) // W
    sink_row = jnp.zeros((1, nq), F32)
    for hh in range(A_GROUP):
        sink_row = jnp.where(head_of_lane == hh, sink_ref[kv * A_GROUP + hh] * LOG2E, sink_row)

    def blk_group(i, carry):
        blocks = [i * ATTN_GROUP + j for j in range(ATTN_GROUP)]
        qcols = [pl.ds(pl.multiple_of(n * W, W), W) for n in blocks]
        wins = [pl.ds(pl.multiple_of(n * W, W), 3 * W) for n in blocks]
        scores = []
        for qc, ws in zip(qcols, wins):
            q4 = qs_ref[:, qc]
            zq = jnp.zeros((dh, W), BF16)
            q2 = jnp.concatenate(
                [jnp.concatenate([jnp.where(kv == 0, q4[hh * dh:(hh + 1) * dh], zq),
                                  jnp.where(kv == 1, q4[hh * dh:(hh + 1) * dh], zq)], axis=0)
                 for hh in range(A_GROUP)], axis=1)
            scores.append(jnp.dot(ks_ref[ws, :], q2, preferred_element_type=F32))
        probs, dens = [], []
        for n, s in zip(blocks, scores):
            sb = jnp.concatenate([s[0:W] + (left_band + jnp.where(n == 0, NEG, 0.0)),
                                  s[W:2 * W],
                                  s[2 * W:3 * W] + (right_band + jnp.where(n == nb - 1, NEG, 0.0))],
                                 axis=0)
            m = jnp.maximum(jnp.max(sb, axis=0, keepdims=True), sink_row)
            p = jnp.exp2(sb - m)
            dens.append(jnp.sum(p, axis=0, keepdims=True) + jnp.exp2(sink_row - m))
            probs.append(p.astype(BF16))
        pvs = [jnp.dot(vs_ref[:, ws], p, preferred_element_type=F32)
               for ws, p in zip(wins, probs)]
        for qc, pv, den in zip(qcols, pvs, dens):
            o = jnp.where(kv == 0, pv[0:dh], pv[dh:2 * dh]) * (1.0 / den)
            for hh in range(A_GROUP):
                out_ref[hh * dh:(hh + 1) * dh, qc] = o[:, hh * W:(hh + 1) * W].astype(BF16)
        return carry

    lax.fori_loop(0, nb // ATTN_GROUP, blk_group, 0)


def _attn_call(zt, sink, cos_t, sin_t, qg, kg, batch, seq):
    n = batch * seq
    qrows = A_GROUP * A_HEAD_DIM
    return pl.pallas_call(
        functools.partial(_attn_kernel, seq=seq),
        grid=(batch, A_KV_HEADS),
        in_specs=[
            pl.BlockSpec(memory_space=pltpu.SMEM),
            pl.BlockSpec((2, LANES, seq), lambda b, g: (TB_AQ // 2 + g, 0, b)),
            pl.BlockSpec((1, LANES, seq), lambda b, g: (TB_AK, 0, b)),
            pl.BlockSpec((1, LANES, seq), lambda b, g: (TB_AV, 0, b)),
            pl.BlockSpec((A_HEAD_DIM, seq), lambda b, g: (0, 0)),
            pl.BlockSpec((A_HEAD_DIM, seq), lambda b, g: (0, 0)),
            pl.BlockSpec((A_HEAD_DIM, LANES), lambda b, g: (0, 0)),
            pl.BlockSpec((A_HEAD_DIM, LANES), lambda b, g: (0, 0)),
        ],
        out_specs=pl.BlockSpec((qrows, seq), lambda b, g: (g, b)),
        out_shape=jax.ShapeDtypeStruct((A_WIDTH, n), BF16),
        scratch_shapes=[
            pltpu.VMEM((qrows, seq), BF16),
            pltpu.VMEM((seq + 2 * WINDOW, LANES), BF16),
            pltpu.VMEM((LANES, seq + 2 * WINDOW), BF16),
        ],
        compiler_params=pltpu.CompilerParams(
            dimension_semantics=("arbitrary", "arbitrary"), vmem_limit_bytes=VMEM_LIMIT_BYTES),
        name="window_attn",
    )(sink, zt, zt, zt, cos_t, sin_t, qg, kg)


def _z_weights(w):
    mq, mk, mv, mo, mg, aq, ak, av, gm = 0, 512, 1024, 1536, 2048, 2064, 2576, 2704, 2832
    parts = [(mq, M_WIDTH), (mv, M_WIDTH), (mo, M_WIDTH), (aq, A_WIDTH),
             (ak, A_KV_HEADS * A_HEAD_DIM), (av, A_KV_HEADS * A_HEAD_DIM),
             (gm, 2 * D_MODEL), (mk, M_WIDTH), (mg, N_GATE)]
    cols = jnp.concatenate([w[:, a:a + n] for a, n in parts], axis=1)
    return jnp.pad(cols, ((0, 0), (0, Z_COLS - cols.shape[1]))).astype(BF16)


def _ffn_weights(w1, w3, w2):
    return w1.astype(BF16), w3.astype(BF16), w2.astype(BF16)


def _rope_tables(seq):
    half = A_HEAD_DIM // 2
    inv = jnp.power(ROPE_THETA, -jnp.arange(half, dtype=F32) / half)
    ang = jnp.arange(seq).astype(F32)[:, None] * inv[None, :]
    return jnp.tile(jnp.cos(ang).T, (2, 1)), jnp.tile(jnp.sin(ang).T, (2, 1))


def _layer(x, p):
    batch, seq, _ = x.shape
    xf = x.reshape(batch * seq, D_MODEL)
    nc = seq // M_CHUNK
    x1, z, zt, gt = _front_call(xf, p["n1"], *p["ffn1"], p["nm"], p["wz"])
    hm = _mlstm_call(z, zt, gt.reshape(N_GATE, batch, nc, M_CHUNK),
                     jnp.broadcast_to(jnp.repeat(p["bg"], nc)[:, None], (N_GATE * nc, LANES)),
                     p["mn"], batch, seq)
    cos_t, sin_t = _rope_tables(seq)
    ha = _attn_call(zt, p["sink"], cos_t, sin_t, p["qn"], p["kn"], batch, seq)
    y = _back_call(x1, z, hm, ha, p["wpm"], p["wpa"], p["wo"], p["n2"], *p["ffn2"])
    return y.reshape(batch, seq, D_MODEL)


def kernel(x_prompt, x_sample, ffn1_norm, ffn1_w1, ffn1_w3, ffn1_w2, mix_norm, w_in, b_gates, m_norm, q_norm, k_norm, sink, w_pm, w_pa, w_out, ffn2_norm, ffn2_w1, ffn2_w3, ffn2_w2):
    depth = w_in.shape[0]
    y_prompt, y_sample = x_prompt, x_sample
    for l in range(depth):
        p = {
            "n1": ffn1_norm[l][None, :],
            "ffn1": _ffn_weights(ffn1_w1[l], ffn1_w3[l], ffn1_w2[l]),
            "nm": mix_norm[l][None, :],
            "wz": _z_weights(w_in[l]),
            "bg": b_gates[l],
            "mn": jnp.broadcast_to(m_norm[l][:, None], (M_WIDTH, LANES)),
            "qn": jnp.broadcast_to(q_norm[l][:, None], (A_HEAD_DIM, LANES)),
            "kn": jnp.broadcast_to(k_norm[l][:, None], (A_HEAD_DIM, LANES)),
            "sink": sink[l],
            "wpm": w_pm[l].astype(BF16),
            "wpa": w_pa[l].astype(BF16),
            "wo": w_out[l].astype(BF16),
            "n2": ffn2_norm[l][None, :],
            "ffn2": _ffn_weights(ffn2_w1[l], ffn2_w3[l], ffn2_w2[l]),
        }
        y_prompt = _layer(y_prompt, p)
        y_sample = _layer(y_sample, p)
    return (y_prompt, y_sample)
```

```python
import functools

import numpy as np
import jax
import jax.numpy as jnp
from jax import lax
from jax.experimental import pallas as pl
from jax.experimental.pallas import tpu as pltpu

F32 = jnp.float32
BF16 = jnp.bfloat16

D_MODEL = 1024
D_FF = 2816
M_HEADS = 4
M_HEAD_DIM = 128
M_WIDTH = 512
N_GATE = 16
A_HEADS = 8
A_KV_HEADS = 2
A_GROUP = 4
A_HEAD_DIM = 64
A_WIDTH = 512
WINDOW = 128
ROPE_THETA = 10000.0
EPS = 1e-6
NEG = -1e30
LOG2E = 1.4426950408889634

LANES = 128
FF_CHUNK = 256
N_FF_CHUNKS = D_FF // FF_CHUNK
Z_CHUNK = 512
N_T_BLOCKS = 18
N_Z_BLOCKS = 20
Z_COLS = (N_T_BLOCKS + N_Z_BLOCKS + 1) * LANES
TB_MQ, TB_MV, TB_MO, TB_AQ, TB_AK, TB_AV = 0, 4, 8, 12, 16, 17
ZB_GM, ZB_GA, ZB_MK = 0, 8, 16
M_CHUNK = 128
ONES_ROWS = 16
VA_ROWS = M_HEAD_DIM + ONES_ROWS
OUT_GROUP = 8
ATTN_GROUP = 8
ATTN_HEADS_PER_UNIT = 2
TOKEN_TILE = 512
SUB_TILES = 2
VMEM_LIMIT_BYTES = 60 * 1024 * 1024


def _rms(x, g):
    return x * lax.rsqrt(jnp.mean(x * x, axis=-1, keepdims=True) + EPS) * g


def _const_spec(shape):
    nd = len(shape)
    return pl.BlockSpec(shape, lambda *_: (0,) * nd, pipeline_mode=pl.Buffered(1))


def _swiglu_into(h_ref, w1_ref, w3_ref, w2_ref, acc_ref):
    def up(c):
        cols = slice(c * FF_CHUNK, (c + 1) * FF_CHUNK)
        h = h_ref[...]
        return (jnp.dot(h, w1_ref[:, cols], preferred_element_type=F32),
                jnp.dot(h, w3_ref[:, cols], preferred_element_type=F32))

    nxt = up(0)
    for c in range(N_FF_CHUNKS):
        a, b = nxt
        if c + 1 < N_FF_CHUNKS:
            nxt = up(c + 1)
        hid = (a * jax.nn.sigmoid(a) * b).astype(BF16)
        part = jnp.dot(hid, w2_ref[c * FF_CHUNK:(c + 1) * FF_CHUNK, :],
                       preferred_element_type=F32)
        if c == 0:
            acc_ref[...] = part
        else:
            acc_ref[...] += part


def _front_kernel(x_ref, n1_ref, w1_ref, w3_ref, w2_ref, nm_ref, wz_ref,
                  x1_ref, z_ref, zt_ref, gt_ref, *scratch):
    hs, h2s, accs = (scratch[k * SUB_TILES:(k + 1) * SUB_TILES] for k in range(3))
    sub = x_ref.shape[0] // SUB_TILES
    rows = [slice(s * sub, (s + 1) * sub) for s in range(SUB_TILES)]

    for s in range(SUB_TILES):
        hs[s][...] = _rms(x_ref[rows[s], :], n1_ref[...]).astype(BF16)
    for s in range(SUB_TILES):
        _swiglu_into(hs[s], w1_ref, w3_ref, w2_ref, accs[s])
    for s in range(SUB_TILES):
        x1 = x_ref[rows[s], :] + 0.5 * accs[s][...]
        x1_ref[rows[s], :] = x1
        h2s[s][...] = _rms(x1, nm_ref[...]).astype(BF16)
    for s in range(SUB_TILES):
        for c0 in range(0, Z_COLS, Z_CHUNK):
            c1 = min(c0 + Z_CHUNK, Z_COLS)
            z = jnp.dot(h2s[s][...], wz_ref[:, c0:c1], preferred_element_type=F32)
            for q in range((c1 - c0) // LANES):
                blk = c0 // LANES + q
                zb = z[:, q * LANES:(q + 1) * LANES]
                if blk < TB_MV:
                    zt_ref[blk, :, rows[s]] = (zb * (M_HEAD_DIM ** -0.5)).T.astype(BF16)
                elif blk < N_T_BLOCKS:
                    zt_ref[blk, :, rows[s]] = zb.T.astype(BF16)
                elif blk < N_T_BLOCKS + N_Z_BLOCKS:
                    z_ref[blk - N_T_BLOCKS, rows[s], :] = zb.astype(BF16)
                else:
                    gt_ref[:, rows[s]] = zb.T[0:N_GATE, :]


def _front_call(x, n1, w1, w3, w2, nm, wz):
    n = x.shape[0]
    tm = TOKEN_TILE

    return pl.pallas_call(
        _front_kernel,
        grid=(n // tm,),
        in_specs=[
            pl.BlockSpec((tm, D_MODEL), lambda i: (i, 0)),
            _const_spec((1, D_MODEL)),
            _const_spec((D_MODEL, D_FF)),
            _const_spec((D_MODEL, D_FF)),
            _const_spec((D_FF, D_MODEL)),
            _const_spec((1, D_MODEL)),
            _const_spec((D_MODEL, Z_COLS)),
        ],
        out_specs=[
            pl.BlockSpec((tm, D_MODEL), lambda i: (i, 0)),
            pl.BlockSpec((N_Z_BLOCKS, tm, LANES), lambda i: (0, i, 0)),
            pl.BlockSpec((N_T_BLOCKS, LANES, tm), lambda i: (0, 0, i)),
            pl.BlockSpec((N_GATE, tm), lambda i: (0, i)),
        ],
        out_shape=[
            jax.ShapeDtypeStruct((n, D_MODEL), F32),
            jax.ShapeDtypeStruct((N_Z_BLOCKS, n, LANES), BF16),
            jax.ShapeDtypeStruct((N_T_BLOCKS, LANES, n), BF16),
            jax.ShapeDtypeStruct((N_GATE, n), F32),
        ],
        scratch_shapes=(
            [pltpu.VMEM((tm // SUB_TILES, D_MODEL), BF16)] * (2 * SUB_TILES)
            + [pltpu.VMEM((tm // SUB_TILES, D_MODEL), F32)] * SUB_TILES),
        compiler_params=pltpu.CompilerParams(
            dimension_semantics=("arbitrary",), vmem_limit_bytes=VMEM_LIMIT_BYTES),
        name="front_ffn_proj",
    )(x, n1, w1, w3, w2, nm, wz)


def _back_kernel(x1_ref, gm_ref, ga_ref, hm_ref, ha_ref, wpm_ref, wpa_ref, wo_ref,
                 n2_ref, w1_ref, w3_ref, w2_ref, y_ref, *scratch):
    hs, h2s, accs, x2s = (scratch[k * SUB_TILES:(k + 1) * SUB_TILES] for k in range(4))
    sub = x1_ref.shape[0] // SUB_TILES
    rows = [slice(s * sub, (s + 1) * sub) for s in range(SUB_TILES)]

    lead = (((0,), (0,)), ((), ()))
    pms = [lax.dot_general(hm_ref[:, rows[s]], wpm_ref[...], lead, preferred_element_type=F32)
           for s in range(SUB_TILES)]
    pas = [lax.dot_general(ha_ref[:, rows[s]], wpa_ref[...], lead, preferred_element_type=F32)
           for s in range(SUB_TILES)]
    for s in range(SUB_TILES):
        for j in range(D_MODEL // LANES):
            sl = slice(j * LANES, (j + 1) * LANES)
            gm = jax.nn.sigmoid(gm_ref[j, rows[s], :].astype(F32))
            ga = jax.nn.sigmoid(ga_ref[j, rows[s], :].astype(F32))
            hs[s][:, sl] = (gm * pms[s][:, sl] + ga * pas[s][:, sl]).astype(BF16)
    for s in range(SUB_TILES):
        x2s[s][...] = x1_ref[rows[s], :] + jnp.dot(hs[s][...], wo_ref[...],
                                                   preferred_element_type=F32)
    for s in range(SUB_TILES):
        h2s[s][...] = _rms(x2s[s][...], n2_ref[...]).astype(BF16)
    for s in range(SUB_TILES):
        _swiglu_into(h2s[s], w1_ref, w3_ref, w2_ref, accs[s])
    for s in range(SUB_TILES):
        y_ref[rows[s], :] = x2s[s][...] + 0.5 * accs[s][...]


def _back_call(x1, z, hm, ha, wpm, wpa, wo, n2, w1, w3, w2):
    n = x1.shape[0]
    tm = TOKEN_TILE
    gblk = D_MODEL // LANES
    return pl.pallas_call(
        _back_kernel,
        grid=(n // tm,),
        in_specs=[
            pl.BlockSpec((tm, D_MODEL), lambda i: (i, 0)),
            pl.BlockSpec((gblk, tm, LANES), lambda i: (ZB_GM // gblk, i, 0)),
            pl.BlockSpec((gblk, tm, LANES), lambda i: (ZB_GA // gblk, i, 0)),
            pl.BlockSpec((M_WIDTH, tm), lambda i: (0, i)),
            pl.BlockSpec((A_WIDTH, tm), lambda i: (0, i)),
            _const_spec((M_WIDTH, D_MODEL)),
            _const_spec((A_WIDTH, D_MODEL)),
            _const_spec((D_MODEL, D_MODEL)),
            _const_spec((1, D_MODEL)),
            _const_spec((D_MODEL, D_FF)),
            _const_spec((D_MODEL, D_FF)),
            _const_spec((D_FF, D_MODEL)),
        ],
        out_specs=pl.BlockSpec((tm, D_MODEL), lambda i: (i, 0)),
        out_shape=jax.ShapeDtypeStruct((n, D_MODEL), F32),
        scratch_shapes=(
            [pltpu.VMEM((tm // SUB_TILES, D_MODEL), BF16)] * (2 * SUB_TILES)
            + [pltpu.VMEM((tm // SUB_TILES, D_MODEL), F32)] * (2 * SUB_TILES)),
        compiler_params=pltpu.CompilerParams(
            dimension_semantics=("arbitrary",), vmem_limit_bytes=VMEM_LIMIT_BYTES),
        name="back_merge_ffn",
    )(x1, z, z, hm, ha, wpm, wpa, wo, n2, w1, w3, w2)


def _split3(x):
    h1 = x.astype(BF16)
    r1 = x - h1.astype(F32)
    h2 = r1.astype(BF16)
    h3 = (r1 - h2.astype(F32)).astype(BF16)
    return h1, h2, h3


def _mlstm_kernel(qt_ref, k_ref, vt_ref, ot_ref, g_ref, bg_ref, mn_ref, out_ref,
                  gi_ref, lf_ref, pf_ref, sf_ref, w_ref, mp_ref, mq_ref,
                  so_ref, sn_ref, dcn_ref, cf_ref, cb_ref, st_ref, *, seq):
    L = M_CHUNK
    nc = seq // L
    hd = pl.program_id(1)
    row = lax.broadcasted_iota(jnp.int32, (L, L), 0)
    col = lax.broadcasted_iota(jnp.int32, (L, L), 1)
    lower = col <= row
    upper = col >= row
    diag = col == row
    dirs = ((upper, cf_ref, pf_ref, 0, 4, L - 1, False),
            (lower, cb_ref, sf_ref, 8, 12, 0, True))

    @pl.when(hd == 0)
    def _():
        gt = g_ref[...].reshape(N_GATE * nc, LANES) + bg_ref[...]
        gi_ref[...] = gt * LOG2E
        lf_ref[...] = (jnp.minimum(gt, 0.0) - jnp.log1p(jnp.exp(-jnp.abs(gt)))) * LOG2E
        h1, h2, h3 = _split3(lf_ref[...])
        incl_prefix = upper.astype(BF16)
        incl_suffix = lower.astype(BF16)
        pf_ref[...] = (jnp.dot(h1, incl_prefix, preferred_element_type=F32)
                       + jnp.dot(h2, incl_prefix, preferred_element_type=F32)
                       + jnp.dot(h3, incl_prefix, preferred_element_type=F32))
        sf_ref[...] = (jnp.dot(h1, incl_suffix, preferred_element_type=F32)
                       + jnp.dot(h2, incl_suffix, preferred_element_type=F32)
                       + jnp.dot(h3, incl_suffix, preferred_element_type=F32))

    ones_rows = jnp.ones((ONES_ROWS, L), BF16)

    def v_aug(cs):
        return jnp.concatenate([vt_ref[0, :, cs], ones_rows], axis=0)

    for d, (_, _, cum_ref, i_off, f_off, g_lane, rev) in enumerate(dirs):
        li = gi_ref[pl.ds(pl.multiple_of((i_off + hd) * nc, 8), nc), :]
        bc = cum_ref[pl.ds(pl.multiple_of((f_off + hd) * nc, 8), nc), :]
        g = jnp.broadcast_to(bc[:, g_lane:g_lane + 1], (nc, LANES))
        a = g + (li - bc)
        m_loc = jnp.broadcast_to(jnp.max(a, axis=1, keepdims=True), (nc, LANES))
        w_ref[d] = jnp.exp2(a - m_loc)
        m = jnp.full((1, LANES), NEG, F32)
        for c in (range(nc - 1, -1, -1) if rev else range(nc)):
            mp_ref[d, c:c + 1, :] = m
            m = jnp.maximum(g[c:c + 1, :] + m, m_loc[c:c + 1, :])
            mq_ref[d, c:c + 1, :] = m
        so_ref[d] = jnp.exp2(g + mp_ref[d] - mq_ref[d])
        sn_ref[d] = jnp.exp2(m_loc - mq_ref[d])

    def inc_group(i, carry):
        chunks = [i * OUT_GROUP + g for g in range(OUT_GROUP)]
        spans = [pl.ds(pl.multiple_of(c * L, L), L) for c in chunks]
        lhs = []
        for c, cs in zip(chunks, spans):
            vat = v_aug(cs).astype(F32)
            lhs.append(jnp.concatenate([vat * w_ref[0, pl.ds(c, 1), :],
                                        vat * w_ref[1, pl.ds(c, 1), :]], axis=0).astype(BF16))
        for c, cs, lh in zip(chunks, spans, lhs):
            dcn_ref[c] = jnp.dot(lh, k_ref[0, cs, :], preferred_element_type=F32)
        return carry

    lax.fori_loop(0, nc // OUT_GROUP, inc_group, 0)

    st_ref[...] = jnp.zeros_like(st_ref)

    def scan_step(i, carry):
        for d, (_, c_ref, _, _, _, _, rev) in enumerate(dirs):
            c = (nc - 1 - i) if rev else i
            st = st_ref[d]
            c_ref[c] = st.astype(BF16)
            st_ref[d] = (so_ref[d, pl.ds(c, 1), :] * st
                         + sn_ref[d, pl.ds(c, 1), :] * dcn_ref[c, d * VA_ROWS:(d + 1) * VA_ROWS, :])
        return carry

    lax.fori_loop(0, nc, scan_step, 0)

    def out_group(i, carry):
        chunks = [i * OUT_GROUP + g for g in range(OUT_GROUP)]
        spans = [pl.ds(pl.multiple_of(c * L, L), L) for c in chunks]
        qtb = [qt_ref[0, :, cs] for cs in spans]
        smt = [jnp.dot(k_ref[0, cs, :], q, preferred_element_type=F32)
               for cs, q in zip(spans, qtb)]
        rhs, floor = [], []
        for c, q, sm in zip(chunks, qtb, smt):
            for d, (mask, _, cum_ref, i_off, f_off, _, _) in enumerate(dirs):
                cum = cum_ref[pl.ds((f_off + hd) * nc + c, 1), :]
                r = gi_ref[pl.ds((i_off + hd) * nc + c, 1), :] - cum
                mp = mp_ref[d, pl.ds(c, 1), :]
                r_col = jnp.sum(jnp.where(diag, r, 0.0), axis=1, keepdims=True)
                rm = jnp.where(mask, r_col, NEG)
                cm = jnp.maximum(jnp.max(rm, axis=0, keepdims=True), mp)
                wq = (jnp.exp2(rm - cm) * sm).astype(BF16)
                sq = q * jnp.exp2(mp - cm).astype(BF16)
                rhs.append(jnp.concatenate([wq, sq], axis=0))
                floor.append(jnp.exp2(-(cum + cm)))
        ys = []
        for g, (c, cs) in enumerate(zip(chunks, spans)):
            vat = v_aug(cs)
            for d, (_, c_ref, _, _, _, _, _) in enumerate(dirs):
                ys.append(jnp.dot(jnp.concatenate([vat, c_ref[c]], axis=1), rhs[2 * g + d],
                                  preferred_element_type=F32))
        for g, cs in enumerate(spans):
            hsum = jnp.zeros((M_HEAD_DIM, L), F32)
            for d in range(2):
                y = ys[2 * g + d]
                den = jnp.maximum(jnp.abs(y[M_HEAD_DIM:M_HEAD_DIM + 1, :]), floor[2 * g + d])
                hsum = hsum + y[0:M_HEAD_DIM, :] * (1.0 / den)
            ms = jnp.sum(hsum * hsum, axis=0, keepdims=True) * (1.0 / M_HEAD_DIM)
            hn = hsum * lax.rsqrt(ms + EPS)
            gate = jax.nn.sigmoid(ot_ref[0, :, cs].astype(F32))
            out_ref[:, cs] = (hn * mn_ref[...] * gate).astype(BF16)
        return carry

    lax.fori_loop(0, nc // OUT_GROUP, out_group, 0)


def _mlstm_call(z, zt, g, bg, mn, batch, seq):
    n = batch * seq
    nc = seq // M_CHUNK
    dk = M_HEAD_DIM

    def tspec(base):
        return pl.BlockSpec((1, LANES, seq), lambda b, h: (base + h, 0, b))

    return pl.pallas_call(
        functools.partial(_mlstm_kernel, seq=seq),
        grid=(batch, M_HEADS),
        in_specs=[
            tspec(TB_MQ),
            pl.BlockSpec((1, seq, LANES), lambda b, h: (ZB_MK + h, b, 0)),
            tspec(TB_MV), tspec(TB_MO),
            pl.BlockSpec((N_GATE, 1, nc, LANES), lambda b, h: (0, b, 0, 0)),
            pl.BlockSpec((N_GATE * nc, LANES), lambda b, h: (0, 0)),
            pl.BlockSpec((dk, LANES), lambda b, h: (h, 0)),
        ],
        out_specs=pl.BlockSpec((dk, seq), lambda b, h: (h, b)),
        out_shape=jax.ShapeDtypeStruct((M_WIDTH, n), BF16),
        scratch_shapes=[
            pltpu.VMEM((nc * N_GATE, LANES), F32),
            pltpu.VMEM((nc * N_GATE, LANES), F32),
            pltpu.VMEM((nc * N_GATE, LANES), F32),
            pltpu.VMEM((nc * N_GATE, LANES), F32),
            pltpu.VMEM((2, nc, LANES), F32),
            pltpu.VMEM((2, nc, LANES), F32),
            pltpu.VMEM((2, nc, LANES), F32),
            pltpu.VMEM((2, nc, LANES), F32),
            pltpu.VMEM((2, nc, LANES), F32),
            pltpu.VMEM((nc, 2 * VA_ROWS, dk), F32),
            pltpu.VMEM((nc, VA_ROWS, dk), BF16),
            pltpu.VMEM((nc, VA_ROWS, dk), BF16),
            pltpu.VMEM((2, VA_ROWS, dk), F32),
        ],
        compiler_params=pltpu.CompilerParams(
            dimension_semantics=("arbitrary", "arbitrary"), vmem_limit_bytes=VMEM_LIMIT_BYTES),
        name="mlstm",
    )(zt, z, zt, zt, g, bg, mn)


def _attn_kernel(sink_ref, qt_ref, kt_ref, vt_ref, cos_ref, sin_ref, qg_ref, kg_ref, out_ref,
                 qs_ref, ks_ref, vs_ref, *, seq):
    W = WINDOW
    nb = seq // W
    kv = pl.program_id(1)
    dh = A_HEAD_DIM
    half = dh // 2

    def swap_halves(a):
        return jnp.concatenate([a[half:], a[:half]], axis=0)

    def rot_half(a):
        return jnp.concatenate([-a[half:], a[:half]], axis=0)

    q_scale = (dh ** -0.5) * LOG2E
    qg, kg = qg_ref[...] * q_scale, kg_ref[...]
    qg_sw, kg_sw = swap_halves(qg), swap_halves(kg)

    def norm_rope(x, ct, st):
        rinv = lax.rsqrt(jnp.sum(x * x, axis=0, keepdims=True) * (1.0 / dh) + EPS)
        return (x * ct + rot_half(x) * st) * rinv

    def prep(i, carry):
        ls = pl.ds(pl.multiple_of(i * W, W), W)
        cs = cos_ref[:, ls]
        sn = sin_ref[:, ls]
        qct, qst = cs * qg, sn * qg_sw
        for j in range(2):
            xq = qt_ref[j, :, ls].astype(F32)
            for hh in range(2):
                r0 = (2 * j + hh) * dh
                qs_ref[r0:r0 + dh, ls] = norm_rope(xq[hh * dh:(hh + 1) * dh], qct, qst).astype(BF16)
        xk = kt_ref[0, pl.ds(pl.multiple_of(kv * dh, dh), dh), ls].astype(F32)
        k2 = jnp.concatenate([norm_rope(xk, cs * kg, sn * kg_sw), jnp.zeros((dh, W), F32)], axis=0)
        ks_ref[pl.ds(pl.multiple_of(W + i * W, W), W), :] = k2.T.astype(BF16)
        return carry

    lax.fori_loop(0, nb, prep, 0, unroll=2)
    ks_ref[0:W, :] = jnp.zeros((W, LANES), BF16)
    ks_ref[seq + W:seq + 2 * W, :] = jnp.zeros((W, LANES), BF16)
    vs_ref[0:dh, 0:W] = jnp.zeros((dh, W), BF16)
    vs_ref[0:dh, seq + W:seq + 2 * W] = jnp.zeros((dh, W), BF16)
    vs_ref[0:dh, W:seq + W] = vt_ref[0, pl.ds(pl.multiple_of(kv * dh, dh), dh), :]
    vs_ref[dh:dh + ONES_ROWS, :] = jnp.ones((ONES_ROWS, seq + 2 * W), BF16)

    hpu = ATTN_HEADS_PER_UNIT
    nq = hpu * W
    kj = lax.broadcasted_iota(jnp.int32, (W, nq), 0)
    qi = lax.broadcasted_iota(jnp.int32, (W, nq), 1) % W
    left_band = jnp.where(kj >= qi, 0.0, NEG)
    right_band = jnp.where(kj <= qi, 0.0, NEG)
    head_of_lane = lax.broadcasted_iota(jnp.int32, (1, nq), 1) // W
    sink_rows = []
    for u in range(A_GROUP // hpu):
        row = jnp.zeros((1, nq), F32)
        for hh in range(hpu):
            row = jnp.where(head_of_lane == hh, sink_ref[kv * A_GROUP + u * hpu + hh] * LOG2E, row)
        sink_rows.append(row)

    def blk_group(i, carry):
        units = [(i * ATTN_GROUP + j, u) for j in range(ATTN_GROUP) for u in range(A_GROUP // hpu)]
        qcols = [pl.ds(pl.multiple_of(n * W, W), W) for n, _ in units]
        wins = [pl.ds(pl.multiple_of(n * W, W), 3 * W) for n, _ in units]
        scores = []
        for (n, u), qc, ws in zip(units, qcols, wins):
            zq = jnp.zeros((dh, W), BF16)
            q2 = jnp.concatenate(
                [jnp.concatenate([qs_ref[(u * hpu + hh) * dh:(u * hpu + hh + 1) * dh, qc], zq], axis=0)
                 for hh in range(hpu)], axis=1)
            scores.append(jnp.dot(ks_ref[ws, :], q2, preferred_element_type=F32))
        probs, dens = [], []
        for (n, u), s in zip(units, scores):
            sb = jnp.concatenate([s[0:W] + (left_band + jnp.where(n == 0, NEG, 0.0)),
                                  s[W:2 * W],
                                  s[2 * W:3 * W] + (right_band + jnp.where(n == nb - 1, NEG, 0.0))],
                                 axis=0)
            m = jnp.maximum(jnp.max(sb, axis=0, keepdims=True), sink_rows[u])
            probs.append(jnp.exp2(sb - m).astype(BF16))
            dens.append(jnp.exp2(sink_rows[u] - m))
        pvs = [jnp.dot(vs_ref[:, ws], p, preferred_element_type=F32)
               for ws, p in zip(wins, probs)]
        for (n, u), qc, pv, sink_term in zip(units, qcols, pvs, dens):
            o = pv[0:dh] * (1.0 / (pv[dh:dh + 1] + sink_term))
            for hh in range(hpu):
                r0 = (u * hpu + hh) * dh
                out_ref[r0:r0 + dh, qc] = o[:, hh * W:(hh + 1) * W].astype(BF16)
        return carry

    lax.fori_loop(0, nb // ATTN_GROUP, blk_group, 0)


def _attn_call(zt, sink, cos_t, sin_t, qg, kg, batch, seq):
    n = batch * seq
    qrows = A_GROUP * A_HEAD_DIM
    return pl.pallas_call(
        functools.partial(_attn_kernel, seq=seq),
        grid=(batch, A_KV_HEADS),
        in_specs=[
            pl.BlockSpec(memory_space=pltpu.SMEM),
            pl.BlockSpec((2, LANES, seq), lambda b, g: (TB_AQ // 2 + g, 0, b)),
            pl.BlockSpec((1, LANES, seq), lambda b, g: (TB_AK, 0, b)),
            pl.BlockSpec((1, LANES, seq), lambda b, g: (TB_AV, 0, b)),
            pl.BlockSpec((A_HEAD_DIM, seq), lambda b, g: (0, 0)),
            pl.BlockSpec((A_HEAD_DIM, seq), lambda b, g: (0, 0)),
            pl.BlockSpec((A_HEAD_DIM, LANES), lambda b, g: (0, 0)),
            pl.BlockSpec((A_HEAD_DIM, LANES), lambda b, g: (0, 0)),
        ],
        out_specs=pl.BlockSpec((qrows, seq), lambda b, g: (g, b)),
        out_shape=jax.ShapeDtypeStruct((A_WIDTH, n), BF16),
        scratch_shapes=[
            pltpu.VMEM((qrows, seq), BF16),
            pltpu.VMEM((seq + 2 * WINDOW, LANES), BF16),
            pltpu.VMEM((A_HEAD_DIM + ONES_ROWS, seq + 2 * WINDOW), BF16),
        ],
        compiler_params=pltpu.CompilerParams(
            dimension_semantics=("arbitrary", "arbitrary"), vmem_limit_bytes=VMEM_LIMIT_BYTES),
        name="window_attn",
    )(sink, zt, zt, zt, cos_t, sin_t, qg, kg)


def _z_weights(w):
    mq, mk, mv, mo, mg, aq, ak, av, gm = 0, 512, 1024, 1536, 2048, 2064, 2576, 2704, 2832
    parts = [(mq, M_WIDTH), (mv, M_WIDTH), (mo, M_WIDTH), (aq, A_WIDTH),
             (ak, A_KV_HEADS * A_HEAD_DIM), (av, A_KV_HEADS * A_HEAD_DIM),
             (gm, 2 * D_MODEL), (mk, M_WIDTH), (mg, N_GATE)]
    cols = jnp.concatenate([w[:, a:a + n] for a, n in parts], axis=1)
    return jnp.pad(cols, ((0, 0), (0, Z_COLS - cols.shape[1]))).astype(BF16)


def _ffn_weights(w1, w3, w2):
    return w1.astype(BF16), w3.astype(BF16), w2.astype(BF16)


def _rope_tables(seq):
    half = A_HEAD_DIM // 2
    inv = jnp.power(ROPE_THETA, -jnp.arange(half, dtype=F32) / half)
    ang = jnp.arange(seq).astype(F32)[:, None] * inv[None, :]
    return jnp.tile(jnp.cos(ang).T, (2, 1)), jnp.tile(jnp.sin(ang).T, (2, 1))


def _layer(x, p):
    batch, seq, _ = x.shape
    xf = x.reshape(batch * seq, D_MODEL)
    nc = seq // M_CHUNK
    x1, z, zt, gt = _front_call(xf, p["n1"], *p["ffn1"], p["nm"], p["wz"])
    hm = _mlstm_call(z, zt, gt.reshape(N_GATE, batch, nc, M_CHUNK),
                     jnp.broadcast_to(jnp.repeat(p["bg"], nc)[:, None], (N_GATE * nc, LANES)),
                     p["mn"], batch, seq)
    cos_t, sin_t = _rope_tables(seq)
    ha = _attn_call(zt, p["sink"], cos_t, sin_t, p["qn"], p["kn"], batch, seq)
    y = _back_call(x1, z, hm, ha, p["wpm"], p["wpa"], p["wo"], p["n2"], *p["ffn2"])
    return y.reshape(batch, seq, D_MODEL)


def kernel(x_prompt, x_sample, ffn1_norm, ffn1_w1, ffn1_w3, ffn1_w2, mix_norm, w_in, b_gates, m_norm, q_norm, k_norm, sink, w_pm, w_pa, w_out, ffn2_norm, ffn2_w1, ffn2_w3, ffn2_w2):
    depth = w_in.shape[0]
    y_prompt, y_sample = x_prompt, x_sample
    for l in range(depth):
        p = {
            "n1": ffn1_norm[l][None, :],
            "ffn1": _ffn_weights(ffn1_w1[l], ffn1_w3[l], ffn1_w2[l]),
            "nm": mix_norm[l][None, :],
            "wz": _z_weights(w_in[l]),
            "bg": b_gates[l],
            "mn": jnp.broadcast_to(m_norm[l][:, None], (M_WIDTH, LANES)),
            "qn": jnp.broadcast_to(q_norm[l][:, None], (A_HEAD_DIM, LANES)),
            "kn": jnp.broadcast_to(k_norm[l][:, None], (A_HEAD_DIM, LANES)),
            "sink": sink[l],
            "wpm": w_pm[l].astype(BF16),
            "wpa": w_pa[l].astype(BF16),
            "wo": w_out[l].astype(BF16),
            "n2": ffn2_norm[l][None, :],
            "ffn2": _ffn_weights(ffn2_w1[l], ffn2_w3[l], ffn2_w2[l]),
        }
        y_prompt = _layer(y_prompt, p)
        y_sample = _layer(y_sample, p)
    return (y_prompt, y_sample)
```

```python
import functools

import numpy as np
import jax
import jax.numpy as jnp
from jax import lax
from jax.experimental import pallas as pl
from jax.experimental.pallas import tpu as pltpu

F32 = jnp.float32
BF16 = jnp.bfloat16

D_MODEL = 1024
D_FF = 2816
M_HEADS = 4
M_HEAD_DIM = 128
M_WIDTH = 512
N_GATE = 16
A_HEADS = 8
A_KV_HEADS = 2
A_GROUP = 4
A_HEAD_DIM = 64
A_WIDTH = 512
WINDOW = 128
ROPE_THETA = 10000.0
EPS = 1e-6
NEG = -1e30
LOG2E = 1.4426950408889634

LANES = 128
FF_CHUNK = 256
N_FF_CHUNKS = D_FF // FF_CHUNK
Z_CHUNK = 512
N_T_BLOCKS = 18
N_Z_BLOCKS = 20
Z_COLS = (N_T_BLOCKS + N_Z_BLOCKS + 1) * LANES
TB_MQ, TB_MV, TB_MO, TB_AQ, TB_AK, TB_AV = 0, 4, 8, 12, 16, 17
ZB_GM, ZB_GA, ZB_MK = 0, 8, 16
M_CHUNK = 128
ONES_ROWS = 16
VA_ROWS = M_HEAD_DIM + ONES_ROWS
OUT_GROUP = 8
ATTN_GROUP = 8
ATTN_HEADS_PER_UNIT = 2
TOKEN_TILE = 512
SUB_TILES = 2
VMEM_LIMIT_BYTES = 60 * 1024 * 1024


def _rms(x, g):
    return x * lax.rsqrt(jnp.mean(x * x, axis=-1, keepdims=True) + EPS) * g


def _const_spec(shape):
    nd = len(shape)
    return pl.BlockSpec(shape, lambda *_: (0,) * nd, pipeline_mode=pl.Buffered(1))


def _swiglu_into(h_ref, w1_ref, w3_ref, w2_ref, acc_ref):
    def up(c):
        cols = slice(c * FF_CHUNK, (c + 1) * FF_CHUNK)
        h = h_ref[...]
        return (jnp.dot(h, w1_ref[:, cols], preferred_element_type=F32),
                jnp.dot(h, w3_ref[:, cols], preferred_element_type=F32))

    nxt = up(0)
    for c in range(N_FF_CHUNKS):
        a, b = nxt
        if c + 1 < N_FF_CHUNKS:
            nxt = up(c + 1)
        hid = (a * jax.nn.sigmoid(a) * b).astype(BF16)
        part = jnp.dot(hid, w2_ref[c * FF_CHUNK:(c + 1) * FF_CHUNK, :],
                       preferred_element_type=F32)
        if c == 0:
            acc_ref[...] = part
        else:
            acc_ref[...] += part


def _front_kernel(x_ref, n1_ref, w1_ref, w3_ref, w2_ref, nm_ref, wz_ref,
                  x1_ref, z_ref, zt_ref, gt_ref, *scratch):
    hs, h2s, accs = (scratch[k * SUB_TILES:(k + 1) * SUB_TILES] for k in range(3))
    sub = x_ref.shape[0] // SUB_TILES
    rows = [slice(s * sub, (s + 1) * sub) for s in range(SUB_TILES)]

    for s in range(SUB_TILES):
        hs[s][...] = _rms(x_ref[rows[s], :], n1_ref[...]).astype(BF16)
    for s in range(SUB_TILES):
        _swiglu_into(hs[s], w1_ref, w3_ref, w2_ref, accs[s])
    for s in range(SUB_TILES):
        x1 = x_ref[rows[s], :] + 0.5 * accs[s][...]
        x1_ref[rows[s], :] = x1
        h2s[s][...] = _rms(x1, nm_ref[...]).astype(BF16)
    for s in range(SUB_TILES):
        for c0 in range(0, Z_COLS, Z_CHUNK):
            c1 = min(c0 + Z_CHUNK, Z_COLS)
            z = jnp.dot(h2s[s][...], wz_ref[:, c0:c1], preferred_element_type=F32)
            for q in range((c1 - c0) // LANES):
                blk = c0 // LANES + q
                zb = z[:, q * LANES:(q + 1) * LANES]
                if blk < TB_MV:
                    zt_ref[blk, :, rows[s]] = (zb * (M_HEAD_DIM ** -0.5)).T.astype(BF16)
                elif blk < N_T_BLOCKS:
                    zt_ref[blk, :, rows[s]] = zb.T.astype(BF16)
                elif blk < N_T_BLOCKS + N_Z_BLOCKS:
                    z_ref[blk - N_T_BLOCKS, rows[s], :] = zb.astype(BF16)
                else:
                    gt_ref[:, rows[s]] = zb.T[0:N_GATE, :]


def _front_call(x, n1, w1, w3, w2, nm, wz):
    n = x.shape[0]
    tm = TOKEN_TILE
    assert n % tm == 0 and tm % (SUB_TILES * LANES) == 0

    return pl.pallas_call(
        _front_kernel,
        grid=(n // tm,),
        in_specs=[
            pl.BlockSpec((tm, D_MODEL), lambda i: (i, 0)),
            _const_spec((1, D_MODEL)),
            _const_spec((D_MODEL, D_FF)),
            _const_spec((D_MODEL, D_FF)),
            _const_spec((D_FF, D_MODEL)),
            _const_spec((1, D_MODEL)),
            _const_spec((D_MODEL, Z_COLS)),
        ],
        out_specs=[
            pl.BlockSpec((tm, D_MODEL), lambda i: (i, 0)),
            pl.BlockSpec((N_Z_BLOCKS, tm, LANES), lambda i: (0, i, 0)),
            pl.BlockSpec((N_T_BLOCKS, LANES, tm), lambda i: (0, 0, i)),
            pl.BlockSpec((N_GATE, tm), lambda i: (0, i)),
        ],
        out_shape=[
            jax.ShapeDtypeStruct((n, D_MODEL), F32),
            jax.ShapeDtypeStruct((N_Z_BLOCKS, n, LANES), BF16),
            jax.ShapeDtypeStruct((N_T_BLOCKS, LANES, n), BF16),
            jax.ShapeDtypeStruct((N_GATE, n), F32),
        ],
        scratch_shapes=(
            [pltpu.VMEM((tm // SUB_TILES, D_MODEL), BF16)] * (2 * SUB_TILES)
            + [pltpu.VMEM((tm // SUB_TILES, D_MODEL), F32)] * SUB_TILES),
        compiler_params=pltpu.CompilerParams(
            dimension_semantics=("arbitrary",), vmem_limit_bytes=VMEM_LIMIT_BYTES),
        name="front_ffn_proj",
    )(x, n1, w1, w3, w2, nm, wz)


def _back_kernel(x1_ref, gm_ref, ga_ref, hm_ref, ha_ref, wpm_ref, wpa_ref, wo_ref,
                 n2_ref, w1_ref, w3_ref, w2_ref, y_ref, *scratch):
    hs, h2s, accs, x2s = (scratch[k * SUB_TILES:(k + 1) * SUB_TILES] for k in range(4))
    sub = x1_ref.shape[0] // SUB_TILES
    rows = [slice(s * sub, (s + 1) * sub) for s in range(SUB_TILES)]

    lead = (((0,), (0,)), ((), ()))
    pms = [lax.dot_general(hm_ref[:, rows[s]], wpm_ref[...], lead, preferred_element_type=F32)
           for s in range(SUB_TILES)]
    pas = [lax.dot_general(ha_ref[:, rows[s]], wpa_ref[...], lead, preferred_element_type=F32)
           for s in range(SUB_TILES)]
    for s in range(SUB_TILES):
        for j in range(D_MODEL // LANES):
            sl = slice(j * LANES, (j + 1) * LANES)
            gm = jax.nn.sigmoid(gm_ref[j, rows[s], :].astype(F32))
            ga = jax.nn.sigmoid(ga_ref[j, rows[s], :].astype(F32))
            hs[s][:, sl] = (gm * pms[s][:, sl] + ga * pas[s][:, sl]).astype(BF16)
    for s in range(SUB_TILES):
        x2s[s][...] = x1_ref[rows[s], :] + jnp.dot(hs[s][...], wo_ref[...],
                                                   preferred_element_type=F32)
    for s in range(SUB_TILES):
        h2s[s][...] = _rms(x2s[s][...], n2_ref[...]).astype(BF16)
    for s in range(SUB_TILES):
        _swiglu_into(h2s[s], w1_ref, w3_ref, w2_ref, accs[s])
    for s in range(SUB_TILES):
        y_ref[rows[s], :] = x2s[s][...] + 0.5 * accs[s][...]


def _back_call(x1, z, hm, ha, wpm, wpa, wo, n2, w1, w3, w2):
    n = x1.shape[0]
    tm = TOKEN_TILE
    assert n % tm == 0 and tm % (SUB_TILES * LANES) == 0
    gblk = D_MODEL // LANES
    return pl.pallas_call(
        _back_kernel,
        grid=(n // tm,),
        in_specs=[
            pl.BlockSpec((tm, D_MODEL), lambda i: (i, 0)),
            pl.BlockSpec((gblk, tm, LANES), lambda i: (ZB_GM // gblk, i, 0)),
            pl.BlockSpec((gblk, tm, LANES), lambda i: (ZB_GA // gblk, i, 0)),
            pl.BlockSpec((M_WIDTH, tm), lambda i: (0, i)),
            pl.BlockSpec((A_WIDTH, tm), lambda i: (0, i)),
            _const_spec((M_WIDTH, D_MODEL)),
            _const_spec((A_WIDTH, D_MODEL)),
            _const_spec((D_MODEL, D_MODEL)),
            _const_spec((1, D_MODEL)),
            _const_spec((D_MODEL, D_FF)),
            _const_spec((D_MODEL, D_FF)),
            _const_spec((D_FF, D_MODEL)),
        ],
        out_specs=pl.BlockSpec((tm, D_MODEL), lambda i: (i, 0)),
        out_shape=jax.ShapeDtypeStruct((n, D_MODEL), F32),
        scratch_shapes=(
            [pltpu.VMEM((tm // SUB_TILES, D_MODEL), BF16)] * (2 * SUB_TILES)
            + [pltpu.VMEM((tm // SUB_TILES, D_MODEL), F32)] * (2 * SUB_TILES)),
        compiler_params=pltpu.CompilerParams(
            dimension_semantics=("arbitrary",), vmem_limit_bytes=VMEM_LIMIT_BYTES),
        name="back_merge_ffn",
    )(x1, z, z, hm, ha, wpm, wpa, wo, n2, w1, w3, w2)


def _split3(x):
    h1 = x.astype(BF16)
    r1 = x - h1.astype(F32)
    h2 = r1.astype(BF16)
    h3 = (r1 - h2.astype(F32)).astype(BF16)
    return h1, h2, h3


def _mlstm_kernel(qt_ref, k_ref, vt_ref, ot_ref, g_ref, bg_ref, mn_ref, out_ref,
                  gi_ref, lf_ref, pf_ref, sf_ref, w_ref, mp_ref, mq_ref,
                  so_ref, sn_ref, dcn_ref, cf_ref, cb_ref, st_ref, *, seq):
    L = M_CHUNK
    nc = seq // L
    hd = pl.program_id(1)
    row = lax.broadcasted_iota(jnp.int32, (L, L), 0)
    col = lax.broadcasted_iota(jnp.int32, (L, L), 1)
    lower = col <= row
    upper = col >= row
    diag = col == row
    dirs = ((upper, cf_ref, pf_ref, 0, 4, L - 1, False),
            (lower, cb_ref, sf_ref, 8, 12, 0, True))

    @pl.when(hd == 0)
    def _():
        gt = g_ref[...].reshape(N_GATE * nc, LANES) + bg_ref[...]
        gi_ref[...] = gt * LOG2E
        lf_ref[...] = (jnp.minimum(gt, 0.0) - jnp.log1p(jnp.exp(-jnp.abs(gt)))) * LOG2E
        h1, h2, h3 = _split3(lf_ref[...])
        incl_prefix = upper.astype(BF16)
        incl_suffix = lower.astype(BF16)
        pf_ref[...] = (jnp.dot(h1, incl_prefix, preferred_element_type=F32)
                       + jnp.dot(h2, incl_prefix, preferred_element_type=F32)
                       + jnp.dot(h3, incl_prefix, preferred_element_type=F32))
        sf_ref[...] = (jnp.dot(h1, incl_suffix, preferred_element_type=F32)
                       + jnp.dot(h2, incl_suffix, preferred_element_type=F32)
                       + jnp.dot(h3, incl_suffix, preferred_element_type=F32))

    ones_rows = jnp.ones((ONES_ROWS, L), BF16)

    def v_aug(cs):
        return jnp.concatenate([vt_ref[0, :, cs], ones_rows], axis=0)

    for d, (_, _, cum_ref, i_off, f_off, g_lane, rev) in enumerate(dirs):
        li = gi_ref[pl.ds(pl.multiple_of((i_off + hd) * nc, 8), nc), :]
        bc = cum_ref[pl.ds(pl.multiple_of((f_off + hd) * nc, 8), nc), :]
        g = jnp.broadcast_to(bc[:, g_lane:g_lane + 1], (nc, LANES))
        a = g + (li - bc)
        m_loc = jnp.broadcast_to(jnp.max(a, axis=1, keepdims=True), (nc, LANES))
        w_ref[d] = jnp.exp2(a - m_loc)
        m = jnp.full((1, LANES), NEG, F32)
        for c in (range(nc - 1, -1, -1) if rev else range(nc)):
            mp_ref[d, c:c + 1, :] = m
            m = jnp.maximum(g[c:c + 1, :] + m, m_loc[c:c + 1, :])
            mq_ref[d, c:c + 1, :] = m
        so_ref[d] = jnp.exp2(g + mp_ref[d] - mq_ref[d])
        sn_ref[d] = jnp.exp2(m_loc - mq_ref[d])

    def inc_group(i, carry):
        chunks = [i * OUT_GROUP + g for g in range(OUT_GROUP)]
        spans = [pl.ds(pl.multiple_of(c * L, L), L) for c in chunks]
        lhs = []
        for c, cs in zip(chunks, spans):
            vat = v_aug(cs).astype(F32)
            lhs.append(jnp.concatenate([vat * w_ref[0, pl.ds(c, 1), :],
                                        vat * w_ref[1, pl.ds(c, 1), :]], axis=0).astype(BF16))
        for c, cs, lh in zip(chunks, spans, lhs):
            dcn_ref[c] = jnp.dot(lh, k_ref[0, cs, :], preferred_element_type=F32)
        return carry

    lax.fori_loop(0, nc // OUT_GROUP, inc_group, 0)

    st_ref[...] = jnp.zeros_like(st_ref)

    def scan_step(i, carry):
        for d, (_, c_ref, _, _, _, _, rev) in enumerate(dirs):
            c = (nc - 1 - i) if rev else i
            st = st_ref[d]
            c_ref[c] = st.astype(BF16)
            st_ref[d] = (so_ref[d, pl.ds(c, 1), :] * st
                         + sn_ref[d, pl.ds(c, 1), :] * dcn_ref[c, d * VA_ROWS:(d + 1) * VA_ROWS, :])
        return carry

    lax.fori_loop(0, nc, scan_step, 0)

    def out_group(i, carry):
        chunks = [i * OUT_GROUP + g for g in range(OUT_GROUP)]
        spans = [pl.ds(pl.multiple_of(c * L, L), L) for c in chunks]
        qtb = [qt_ref[0, :, cs] for cs in spans]
        smt = [jnp.dot(k_ref[0, cs, :], q, preferred_element_type=F32)
               for cs, q in zip(spans, qtb)]
        rhs, floor = [], []
        for c, q, sm in zip(chunks, qtb, smt):
            for d, (mask, _, cum_ref, i_off, f_off, _, _) in enumerate(dirs):
                cum = cum_ref[pl.ds((f_off + hd) * nc + c, 1), :]
                r = gi_ref[pl.ds((i_off + hd) * nc + c, 1), :] - cum
                mp = mp_ref[d, pl.ds(c, 1), :]
                r_col = jnp.sum(jnp.where(diag, r, 0.0), axis=1, keepdims=True)
                rm = jnp.where(mask, r_col, NEG)
                cm = jnp.maximum(jnp.max(rm, axis=0, keepdims=True), mp)
                wq = (jnp.exp2(rm - cm) * sm).astype(BF16)
                sq = q * jnp.exp2(mp - cm).astype(BF16)
                rhs.append(jnp.concatenate([wq, sq], axis=0))
                floor.append(jnp.exp2(-(cum + cm)))
        ys = []
        for g, (c, cs) in enumerate(zip(chunks, spans)):
            vat = v_aug(cs)
            for d, (_, c_ref, _, _, _, _, _) in enumerate(dirs):
                ys.append(jnp.dot(jnp.concatenate([vat, c_ref[c]], axis=1), rhs[2 * g + d],
                                  preferred_element_type=F32))
        for g, cs in enumerate(spans):
            hsum = jnp.zeros((M_HEAD_DIM, L), F32)
            for d in range(2):
                y = ys[2 * g + d]
                den = jnp.maximum(jnp.abs(y[M_HEAD_DIM:M_HEAD_DIM + 1, :]), floor[2 * g + d])
                hsum = hsum + y[0:M_HEAD_DIM, :] * (1.0 / den)
            ms = jnp.sum(hsum * hsum, axis=0, keepdims=True) * (1.0 / M_HEAD_DIM)
            hn = hsum * lax.rsqrt(ms + EPS)
            gate = jax.nn.sigmoid(ot_ref[0, :, cs].astype(F32))
            out_ref[:, cs] = (hn * mn_ref[...] * gate).astype(BF16)
        return carry

    lax.fori_loop(0, nc // OUT_GROUP, out_group, 0)


def _mlstm_call(z, zt, g, bg, mn, batch, seq):
    n = batch * seq
    nc = seq // M_CHUNK
    dk = M_HEAD_DIM
    assert seq % (M_CHUNK * OUT_GROUP) == 0

    def tspec(base):
        return pl.BlockSpec((1, LANES, seq), lambda b, h: (base + h, 0, b))

    return pl.pallas_call(
        functools.partial(_mlstm_kernel, seq=seq),
        grid=(batch, M_HEADS),
        in_specs=[
            tspec(TB_MQ),
            pl.BlockSpec((1, seq, LANES), lambda b, h: (ZB_MK + h, b, 0)),
            tspec(TB_MV), tspec(TB_MO),
            pl.BlockSpec((N_GATE, 1, nc, LANES), lambda b, h: (0, b, 0, 0)),
            pl.BlockSpec((N_GATE * nc, LANES), lambda b, h: (0, 0)),
            pl.BlockSpec((dk, LANES), lambda b, h: (h, 0)),
        ],
        out_specs=pl.BlockSpec((dk, seq), lambda b, h: (h, b)),
        out_shape=jax.ShapeDtypeStruct((M_WIDTH, n), BF16),
        scratch_shapes=[
            pltpu.VMEM((nc * N_GATE, LANES), F32),
            pltpu.VMEM((nc * N_GATE, LANES), F32),
            pltpu.VMEM((nc * N_GATE, LANES), F32),
            pltpu.VMEM((nc * N_GATE, LANES), F32),
            pltpu.VMEM((2, nc, LANES), F32),
            pltpu.VMEM((2, nc, LANES), F32),
            pltpu.VMEM((2, nc, LANES), F32),
            pltpu.VMEM((2, nc, LANES), F32),
            pltpu.VMEM((2, nc, LANES), F32),
            pltpu.VMEM((nc, 2 * VA_ROWS, dk), F32),
            pltpu.VMEM((nc, VA_ROWS, dk), BF16),
            pltpu.VMEM((nc, VA_ROWS, dk), BF16),
            pltpu.VMEM((2, VA_ROWS, dk), F32),
        ],
        compiler_params=pltpu.CompilerParams(
            dimension_semantics=("arbitrary", "arbitrary"), vmem_limit_bytes=VMEM_LIMIT_BYTES),
        name="mlstm",
    )(zt, z, zt, zt, g, bg, mn)


def _attn_kernel(sink_ref, qt_ref, kt_ref, vt_ref, cos_ref, sin_ref, qg_ref, kg_ref, out_ref,
                 qs_ref, ks_ref, vs_ref, *, seq):
    W = WINDOW
    nb = seq // W
    kv = pl.program_id(1)
    dh = A_HEAD_DIM
    half = dh // 2

    def swap_halves(a):
        return jnp.concatenate([a[half:], a[:half]], axis=0)

    def rot_gain(g):
        return jnp.concatenate([-g[half:], g[:half]], axis=0)

    q_scale = (dh ** -0.5) * LOG2E
    qg, kg = qg_ref[...] * q_scale, kg_ref[...]
    qg_sw, kg_sw = rot_gain(qg), rot_gain(kg)

    def norm_rope(x, ct, st):
        rinv = lax.rsqrt(jnp.sum(x * x, axis=0, keepdims=True) * (1.0 / dh) + EPS)
        return (x * ct + swap_halves(x) * st) * rinv

    def prep(i, carry):
        ls = pl.ds(pl.multiple_of(i * W, W), W)
        cs = cos_ref[:, ls]
        sn = sin_ref[:, ls]
        qct, qst = cs * qg, sn * qg_sw
        for j in range(2):
            xq = qt_ref[j, :, ls].astype(F32)
            for hh in range(2):
                r0 = (2 * j + hh) * dh
                qs_ref[r0:r0 + dh, ls] = norm_rope(xq[hh * dh:(hh + 1) * dh], qct, qst).astype(BF16)
        xk = kt_ref[0, pl.ds(pl.multiple_of(kv * dh, dh), dh), ls].astype(F32)
        k2 = jnp.concatenate([norm_rope(xk, cs * kg, sn * kg_sw), jnp.zeros((dh, W), F32)], axis=0)
        ks_ref[pl.ds(pl.multiple_of(W + i * W, W), W), :] = k2.T.astype(BF16)
        return carry

    lax.fori_loop(0, nb, prep, 0, unroll=4)
    ks_ref[0:W, :] = jnp.zeros((W, LANES), BF16)
    ks_ref[seq + W:seq + 2 * W, :] = jnp.zeros((W, LANES), BF16)
    vs_ref[0:dh, 0:W] = jnp.zeros((dh, W), BF16)
    vs_ref[0:dh, seq + W:seq + 2 * W] = jnp.zeros((dh, W), BF16)
    vs_ref[0:dh, W:seq + W] = vt_ref[0, pl.ds(pl.multiple_of(kv * dh, dh), dh), :]
    vs_ref[dh:dh + ONES_ROWS, :] = jnp.ones((ONES_ROWS, seq + 2 * W), BF16)

    hpu = ATTN_HEADS_PER_UNIT
    nq = hpu * W
    kj = lax.broadcasted_iota(jnp.int32, (W, nq), 0)
    qi = lax.broadcasted_iota(jnp.int32, (W, nq), 1) % W
    left_band = jnp.where(kj >= qi, 0.0, NEG)
    right_band = jnp.where(kj <= qi, 0.0, NEG)
    head_of_lane = lax.broadcasted_iota(jnp.int32, (1, nq), 1) // W
    sink_rows = []
    for u in range(A_GROUP // hpu):
        row = jnp.zeros((1, nq), F32)
        for hh in range(hpu):
            row = jnp.where(head_of_lane == hh, sink_ref[kv * A_GROUP + u * hpu + hh] * LOG2E, row)
        sink_rows.append(row)

    def blk_group(i, carry):
        units = [(i * ATTN_GROUP + j, u) for j in range(ATTN_GROUP) for u in range(A_GROUP // hpu)]
        qcols = [pl.ds(pl.multiple_of(n * W, W), W) for n, _ in units]
        wins = [pl.ds(pl.multiple_of(n * W, W), 3 * W) for n, _ in units]
        scores = []
        for (n, u), qc, ws in zip(units, qcols, wins):
            zq = jnp.zeros((dh, W), BF16)
            q2 = jnp.concatenate(
                [jnp.concatenate([qs_ref[(u * hpu + hh) * dh:(u * hpu + hh + 1) * dh, qc], zq], axis=0)
                 for hh in range(hpu)], axis=1)
            scores.append(jnp.dot(ks_ref[ws, :], q2, preferred_element_type=F32))
        probs, dens = [], []
        for (n, u), s in zip(units, scores):
            sb = jnp.concatenate([s[0:W] + (left_band + jnp.where(n == 0, NEG, 0.0)),
                                  s[W:2 * W],
                                  s[2 * W:3 * W] + (right_band + jnp.where(n == nb - 1, NEG, 0.0))],
                                 axis=0)
            m = jnp.maximum(jnp.max(sb, axis=0, keepdims=True), sink_rows[u])
            probs.append(jnp.exp2(sb - m).astype(BF16))
            dens.append(jnp.exp2(sink_rows[u] - m))
        pvs = [jnp.dot(vs_ref[:, ws], p, preferred_element_type=F32)
               for ws, p in zip(wins, probs)]
        for (n, u), qc, pv, sink_term in zip(units, qcols, pvs, dens):
            o = pv[0:dh] * (1.0 / (pv[dh:dh + 1] + sink_term))
            for hh in range(hpu):
                r0 = (u * hpu + hh) * dh
                out_ref[r0:r0 + dh, qc] = o[:, hh * W:(hh + 1) * W].astype(BF16)
        return carry

    lax.fori_loop(0, nb // ATTN_GROUP, blk_group, 0)


def _attn_call(zt, sink, cos_t, sin_t, qg, kg, batch, seq):
    n = batch * seq
    qrows = A_GROUP * A_HEAD_DIM
    assert seq % (WINDOW * ATTN_GROUP) == 0 and A_GROUP % ATTN_HEADS_PER_UNIT == 0
    return pl.pallas_call(
        functools.partial(_attn_kernel, seq=seq),
        grid=(batch, A_KV_HEADS),
        in_specs=[
            pl.BlockSpec(memory_space=pltpu.SMEM),
            pl.BlockSpec((2, LANES, seq), lambda b, g: (TB_AQ // 2 + g, 0, b)),
            pl.BlockSpec((1, LANES, seq), lambda b, g: (TB_AK, 0, b)),
            pl.BlockSpec((1, LANES, seq), lambda b, g: (TB_AV, 0, b)),
            pl.BlockSpec((A_HEAD_DIM, seq), lambda b, g: (0, 0)),
            pl.BlockSpec((A_HEAD_DIM, seq), lambda b, g: (0, 0)),
            pl.BlockSpec((A_HEAD_DIM, LANES), lambda b, g: (0, 0)),
            pl.BlockSpec((A_HEAD_DIM, LANES), lambda b, g: (0, 0)),
        ],
        out_specs=pl.BlockSpec((qrows, seq), lambda b, g: (g, b)),
        out_shape=jax.ShapeDtypeStruct((A_WIDTH, n), BF16),
        scratch_shapes=[
            pltpu.VMEM((qrows, seq), BF16),
            pltpu.VMEM((seq + 2 * WINDOW, LANES), BF16),
            pltpu.VMEM((A_HEAD_DIM + ONES_ROWS, seq + 2 * WINDOW), BF16),
        ],
        compiler_params=pltpu.CompilerParams(
            dimension_semantics=("arbitrary", "arbitrary"), vmem_limit_bytes=VMEM_LIMIT_BYTES),
        name="window_attn",
    )(sink, zt, zt, zt, cos_t, sin_t, qg, kg)


def _z_weights(w):
    mq, mk, mv, mo, mg, aq, ak, av, gm = 0, 512, 1024, 1536, 2048, 2064, 2576, 2704, 2832
    parts = [(mq, M_WIDTH), (mv, M_WIDTH), (mo, M_WIDTH), (aq, A_WIDTH),
             (ak, A_KV_HEADS * A_HEAD_DIM), (av, A_KV_HEADS * A_HEAD_DIM),
             (gm, 2 * D_MODEL), (mk, M_WIDTH), (mg, N_GATE)]
    cols = jnp.concatenate([w[:, a:a + n] for a, n in parts], axis=1)
    return jnp.pad(cols, ((0, 0), (0, Z_COLS - cols.shape[1]))).astype(BF16)


def _ffn_weights(w1, w3, w2):
    return w1.astype(BF16), w3.astype(BF16), w2.astype(BF16)


def _rope_tables(seq):
    half = A_HEAD_DIM // 2
    inv = jnp.power(ROPE_THETA, -jnp.arange(half, dtype=F32) / half)
    ang = jnp.arange(seq).astype(F32)[:, None] * inv[None, :]
    return jnp.tile(jnp.cos(ang).T, (2, 1)), jnp.tile(jnp.sin(ang).T, (2, 1))


def _layer(x, p):
    batch, seq, _ = x.shape
    xf = x.reshape(batch * seq, D_MODEL)
    nc = seq // M_CHUNK
    x1, z, zt, gt = _front_call(xf, p["n1"], *p["ffn1"], p["nm"], p["wz"])
    hm = _mlstm_call(z, zt, gt.reshape(N_GATE, batch, nc, M_CHUNK),
                     jnp.broadcast_to(jnp.repeat(p["bg"], nc)[:, None], (N_GATE * nc, LANES)),
                     p["mn"], batch, seq)
    cos_t, sin_t = _rope_tables(seq)
    ha = _attn_call(zt, p["sink"], cos_t, sin_t, p["qn"], p["kn"], batch, seq)
    y = _back_call(x1, z, hm, ha, p["wpm"], p["wpa"], p["wo"], p["n2"], *p["ffn2"])
    return y.reshape(batch, seq, D_MODEL)


def kernel(x_prompt, x_sample, ffn1_norm, ffn1_w1, ffn1_w3, ffn1_w2, mix_norm, w_in, b_gates, m_norm, q_norm, k_norm, sink, w_pm, w_pa, w_out, ffn2_norm, ffn2_w1, ffn2_w3, ffn2_w2):
    depth = w_in.shape[0]
    y_prompt, y_sample = x_prompt, x_sample
    for l in range(depth):
        p = {
            "n1": ffn1_norm[l][None, :],
            "ffn1": _ffn_weights(ffn1_w1[l], ffn1_w3[l], ffn1_w2[l]),
            "nm": mix_norm[l][None, :],
            "wz": _z_weights(w_in[l]),
            "bg": b_gates[l],
            "mn": jnp.broadcast_to(m_norm[l][:, None], (M_WIDTH, LANES)),
            "qn": jnp.broadcast_to(q_norm[l][:, None], (A_HEAD_DIM, LANES)),
            "kn": jnp.broadcast_to(k_norm[l][:, None], (A_HEAD_DIM, LANES)),
            "sink": sink[l],
            "wpm": w_pm[l].astype(BF16),
            "wpa": w_pa[l].astype(BF16),
            "wo": w_out[l].astype(BF16),
            "n2": ffn2_norm[l][None, :],
            "ffn2": _ffn_weights(ffn2_w1[l], ffn2_w3[l], ffn2_w2[l]),
        }
        y_prompt = _layer(y_prompt, p)
        y_sample = _layer(y_sample, p)
    return (y_prompt, y_sample)
```

```python
import functools

import numpy as np
import jax
import jax.numpy as jnp
from jax import lax
from jax.experimental import pallas as pl
from jax.experimental.pallas import tpu as pltpu

F32 = jnp.float32
BF16 = jnp.bfloat16

D_MODEL = 1024
D_FF = 2816
M_HEADS = 4
M_HEAD_DIM = 128
M_WIDTH = 512
N_GATE = 16
A_HEADS = 8
A_KV_HEADS = 2
A_GROUP = 4
A_HEAD_DIM = 64
A_WIDTH = 512
WINDOW = 128
ROPE_THETA = 10000.0
EPS = 1e-6
NEG = -1e30
LOG2E = 1.4426950408889634

LANES = 128
FF_CHUNK = 256
N_FF_CHUNKS = D_FF // FF_CHUNK
Z_CHUNK = 512
N_T_BLOCKS = 18
N_Z_BLOCKS = 20
Z_COLS = (N_T_BLOCKS + N_Z_BLOCKS + 1) * LANES
TB_MQ, TB_MV, TB_MO, TB_AQ, TB_AK, TB_AV = 0, 4, 8, 12, 16, 17
ZB_GM, ZB_GA, ZB_MK = 0, 8, 16
M_CHUNK = 128
ONES_ROWS = 16
VA_ROWS = M_HEAD_DIM + ONES_ROWS
OUT_GROUP = 16
ATTN_GROUP = 16
ATTN_HEADS_PER_UNIT = 2
TOKEN_TILE = 512
SUB_TILES = 2
VMEM_LIMIT_BYTES = 60 * 1024 * 1024


def _rms(x, g):
    return x * lax.rsqrt(jnp.mean(x * x, axis=-1, keepdims=True) + EPS) * g


def _const_spec(shape):
    nd = len(shape)
    return pl.BlockSpec(shape, lambda *_: (0,) * nd, pipeline_mode=pl.Buffered(1))


def _swiglu_into(h_ref, w1_ref, w3_ref, w2_ref, acc_ref):
    def up(c):
        cols = slice(c * FF_CHUNK, (c + 1) * FF_CHUNK)
        h = h_ref[...]
        return (jnp.dot(h, w1_ref[:, cols], preferred_element_type=F32),
                jnp.dot(h, w3_ref[:, cols], preferred_element_type=F32))

    nxt = up(0)
    for c in range(N_FF_CHUNKS):
        a, b = nxt
        if c + 1 < N_FF_CHUNKS:
            nxt = up(c + 1)
        hid = (a * jax.nn.sigmoid(a) * b).astype(BF16)
        part = jnp.dot(hid, w2_ref[c * FF_CHUNK:(c + 1) * FF_CHUNK, :],
                       preferred_element_type=F32)
        if c == 0:
            acc_ref[...] = part
        else:
            acc_ref[...] += part


def _front_kernel(x_ref, n1_ref, w1_ref, w3_ref, w2_ref, nm_ref, wz_ref,
                  x1_ref, z_ref, zt_ref, gt_ref, *scratch):
    hs, h2s, accs = (scratch[k * SUB_TILES:(k + 1) * SUB_TILES] for k in range(3))
    sub = x_ref.shape[0] // SUB_TILES
    rows = [slice(s * sub, (s + 1) * sub) for s in range(SUB_TILES)]

    for s in range(SUB_TILES):
        hs[s][...] = _rms(x_ref[rows[s], :], n1_ref[...]).astype(BF16)
    for s in range(SUB_TILES):
        _swiglu_into(hs[s], w1_ref, w3_ref, w2_ref, accs[s])
    for s in range(SUB_TILES):
        x1 = x_ref[rows[s], :] + 0.5 * accs[s][...]
        x1_ref[rows[s], :] = x1
        h2s[s][...] = _rms(x1, nm_ref[...]).astype(BF16)
    for s in range(SUB_TILES):
        for c0 in range(0, Z_COLS, Z_CHUNK):
            c1 = min(c0 + Z_CHUNK, Z_COLS)
            z = jnp.dot(h2s[s][...], wz_ref[:, c0:c1], preferred_element_type=F32)
            for q in range((c1 - c0) // LANES):
                blk = c0 // LANES + q
                zb = z[:, q * LANES:(q + 1) * LANES]
                if blk < TB_MV:
                    zt_ref[blk, :, rows[s]] = (zb * (M_HEAD_DIM ** -0.5)).T.astype(BF16)
                elif blk < N_T_BLOCKS:
                    zt_ref[blk, :, rows[s]] = zb.T.astype(BF16)
                elif blk < N_T_BLOCKS + N_Z_BLOCKS:
                    z_ref[blk - N_T_BLOCKS, rows[s], :] = zb.astype(BF16)
                else:
                    gt_ref[:, rows[s]] = zb.T[0:N_GATE, :]


def _front_call(x, n1, w1, w3, w2, nm, wz):
    n = x.shape[0]
    tm = TOKEN_TILE
    assert n % tm == 0 and tm % (SUB_TILES * LANES) == 0

    return pl.pallas_call(
        _front_kernel,
        grid=(n // tm,),
        in_specs=[
            pl.BlockSpec((tm, D_MODEL), lambda i: (i, 0)),
            _const_spec((1, D_MODEL)),
            _const_spec((D_MODEL, D_FF)),
            _const_spec((D_MODEL, D_FF)),
            _const_spec((D_FF, D_MODEL)),
            _const_spec((1, D_MODEL)),
            _const_spec((D_MODEL, Z_COLS)),
        ],
        out_specs=[
            pl.BlockSpec((tm, D_MODEL), lambda i: (i, 0)),
            pl.BlockSpec((N_Z_BLOCKS, tm, LANES), lambda i: (0, i, 0)),
            pl.BlockSpec((N_T_BLOCKS, LANES, tm), lambda i: (0, 0, i)),
            pl.BlockSpec((N_GATE, tm), lambda i: (0, i)),
        ],
        out_shape=[
            jax.ShapeDtypeStruct((n, D_MODEL), F32),
            jax.ShapeDtypeStruct((N_Z_BLOCKS, n, LANES), BF16),
            jax.ShapeDtypeStruct((N_T_BLOCKS, LANES, n), BF16),
            jax.ShapeDtypeStruct((N_GATE, n), F32),
        ],
        scratch_shapes=(
            [pltpu.VMEM((tm // SUB_TILES, D_MODEL), BF16)] * (2 * SUB_TILES)
            + [pltpu.VMEM((tm // SUB_TILES, D_MODEL), F32)] * SUB_TILES),
        compiler_params=pltpu.CompilerParams(
            dimension_semantics=("arbitrary",), vmem_limit_bytes=VMEM_LIMIT_BYTES),
        name="front_ffn_proj",
    )(x, n1, w1, w3, w2, nm, wz)


def _back_kernel(x1_ref, gm_ref, ga_ref, hm_ref, ha_ref, wpm_ref, wpa_ref, wo_ref,
                 n2_ref, w1_ref, w3_ref, w2_ref, y_ref, *scratch):
    hs, h2s, accs, x2s = (scratch[k * SUB_TILES:(k + 1) * SUB_TILES] for k in range(4))
    sub = x1_ref.shape[0] // SUB_TILES
    rows = [slice(s * sub, (s + 1) * sub) for s in range(SUB_TILES)]

    lead = (((0,), (0,)), ((), ()))
    pms = [lax.dot_general(hm_ref[:, rows[s]], wpm_ref[...], lead, preferred_element_type=F32)
           for s in range(SUB_TILES)]
    pas = [lax.dot_general(ha_ref[:, rows[s]], wpa_ref[...], lead, preferred_element_type=F32)
           for s in range(SUB_TILES)]
    for s in range(SUB_TILES):
        for j in range(D_MODEL // LANES):
            sl = slice(j * LANES, (j + 1) * LANES)
            gm = jax.nn.sigmoid(gm_ref[j, rows[s], :].astype(F32))
            ga = jax.nn.sigmoid(ga_ref[j, rows[s], :].astype(F32))
            hs[s][:, sl] = (gm * pms[s][:, sl] + ga * pas[s][:, sl]).astype(BF16)
    for s in range(SUB_TILES):
        x2s[s][...] = x1_ref[rows[s], :] + jnp.dot(hs[s][...], wo_ref[...],
                                                   preferred_element_type=F32)
    for s in range(SUB_TILES):
        h2s[s][...] = _rms(x2s[s][...], n2_ref[...]).astype(BF16)
    for s in range(SUB_TILES):
        _swiglu_into(h2s[s], w1_ref, w3_ref, w2_ref, accs[s])
    for s in range(SUB_TILES):
        y_ref[rows[s], :] = x2s[s][...] + 0.5 * accs[s][...]


def _back_call(x1, z, hm, ha, wpm, wpa, wo, n2, w1, w3, w2):
    n = x1.shape[0]
    tm = TOKEN_TILE
    assert n % tm == 0 and tm % (SUB_TILES * LANES) == 0
    gblk = D_MODEL // LANES
    return pl.pallas_call(
        _back_kernel,
        grid=(n // tm,),
        in_specs=[
            pl.BlockSpec((tm, D_MODEL), lambda i: (i, 0)),
            pl.BlockSpec((gblk, tm, LANES), lambda i: (ZB_GM // gblk, i, 0)),
            pl.BlockSpec((gblk, tm, LANES), lambda i: (ZB_GA // gblk, i, 0)),
            pl.BlockSpec((M_WIDTH, tm), lambda i: (0, i)),
            pl.BlockSpec((A_WIDTH, tm), lambda i: (0, i)),
            _const_spec((M_WIDTH, D_MODEL)),
            _const_spec((A_WIDTH, D_MODEL)),
            _const_spec((D_MODEL, D_MODEL)),
            _const_spec((1, D_MODEL)),
            _const_spec((D_MODEL, D_FF)),
            _const_spec((D_MODEL, D_FF)),
            _const_spec((D_FF, D_MODEL)),
        ],
        out_specs=pl.BlockSpec((tm, D_MODEL), lambda i: (i, 0)),
        out_shape=jax.ShapeDtypeStruct((n, D_MODEL), F32),
        scratch_shapes=(
            [pltpu.VMEM((tm // SUB_TILES, D_MODEL), BF16)] * (2 * SUB_TILES)
            + [pltpu.VMEM((tm // SUB_TILES, D_MODEL), F32)] * (2 * SUB_TILES)),
        compiler_params=pltpu.CompilerParams(
            dimension_semantics=("arbitrary",), vmem_limit_bytes=VMEM_LIMIT_BYTES),
        name="back_merge_ffn",
    )(x1, z, z, hm, ha, wpm, wpa, wo, n2, w1, w3, w2)


def _split3(x):
    h1 = x.astype(BF16)
    r1 = x - h1.astype(F32)
    h2 = r1.astype(BF16)
    h3 = (r1 - h2.astype(F32)).astype(BF16)
    return h1, h2, h3


def _mlstm_kernel(qt_ref, k_ref, vt_ref, ot_ref, g_ref, bg_ref, mn_ref, out_ref,
                  gi_ref, lf_ref, pf_ref, sf_ref, w_ref, mp_ref, mq_ref,
                  so_ref, sn_ref, dcn_ref, cf_ref, cb_ref, st_ref, *, seq):
    L = M_CHUNK
    nc = seq // L
    hd = pl.program_id(1)
    row = lax.broadcasted_iota(jnp.int32, (L, L), 0)
    col = lax.broadcasted_iota(jnp.int32, (L, L), 1)
    lower = col <= row
    upper = col >= row
    diag = col == row
    dirs = ((upper, cf_ref, pf_ref, 0, 4, L - 1, False),
            (lower, cb_ref, sf_ref, 8, 12, 0, True))

    @pl.when(hd == 0)
    def _():
        gt = g_ref[...].reshape(N_GATE * nc, LANES) + bg_ref[...]
        gi_ref[...] = gt * LOG2E
        lf_ref[...] = (jnp.minimum(gt, 0.0) - jnp.log1p(jnp.exp(-jnp.abs(gt)))) * LOG2E
        h1, h2, h3 = _split3(lf_ref[...])
        incl_prefix = upper.astype(BF16)
        incl_suffix = lower.astype(BF16)
        pf_ref[...] = (jnp.dot(h1, incl_prefix, preferred_element_type=F32)
                       + jnp.dot(h2, incl_prefix, preferred_element_type=F32)
                       + jnp.dot(h3, incl_prefix, preferred_element_type=F32))
        sf_ref[...] = (jnp.dot(h1, incl_suffix, preferred_element_type=F32)
                       + jnp.dot(h2, incl_suffix, preferred_element_type=F32)
                       + jnp.dot(h3, incl_suffix, preferred_element_type=F32))

    ones_rows = jnp.ones((ONES_ROWS, L), BF16)

    def v_aug(cs):
        return jnp.concatenate([vt_ref[0, :, cs], ones_rows], axis=0)

    for d, (_, _, cum_ref, i_off, f_off, g_lane, rev) in enumerate(dirs):
        li = gi_ref[pl.ds(pl.multiple_of((i_off + hd) * nc, 8), nc), :]
        bc = cum_ref[pl.ds(pl.multiple_of((f_off + hd) * nc, 8), nc), :]
        g = jnp.broadcast_to(bc[:, g_lane:g_lane + 1], (nc, LANES))
        a = g + (li - bc)
        m_loc = jnp.broadcast_to(jnp.max(a, axis=1, keepdims=True), (nc, LANES))
        w_ref[d] = jnp.exp2(a - m_loc)
        m = jnp.full((1, LANES), NEG, F32)
        for c in (range(nc - 1, -1, -1) if rev else range(nc)):
            mp_ref[d, c:c + 1, :] = m
            m = jnp.maximum(g[c:c + 1, :] + m, m_loc[c:c + 1, :])
            mq_ref[d, c:c + 1, :] = m
        so_ref[d] = jnp.exp2(g + mp_ref[d] - mq_ref[d])
        sn_ref[d] = jnp.exp2(m_loc - mq_ref[d])

    def inc_group(i, carry):
        chunks = [i * OUT_GROUP + g for g in range(OUT_GROUP)]
        spans = [pl.ds(pl.multiple_of(c * L, L), L) for c in chunks]
        lhs = []
        for c, cs in zip(chunks, spans):
            vat = v_aug(cs).astype(F32)
            lhs.append(jnp.concatenate([vat * w_ref[0, pl.ds(c, 1), :],
                                        vat * w_ref[1, pl.ds(c, 1), :]], axis=0).astype(BF16))
        for c, cs, lh in zip(chunks, spans, lhs):
            dcn_ref[c] = jnp.dot(lh, k_ref[0, cs, :], preferred_element_type=F32)
        return carry

    lax.fori_loop(0, nc // OUT_GROUP, inc_group, 0)

    st_ref[...] = jnp.zeros_like(st_ref)

    def scan_step(i, carry):
        for d, (_, c_ref, _, _, _, _, rev) in enumerate(dirs):
            c = (nc - 1 - i) if rev else i
            st = st_ref[d]
            c_ref[c] = st.astype(BF16)
            st_ref[d] = (so_ref[d, pl.ds(c, 1), :] * st
                         + sn_ref[d, pl.ds(c, 1), :] * dcn_ref[c, d * VA_ROWS:(d + 1) * VA_ROWS, :])
        return carry

    lax.fori_loop(0, nc, scan_step, 0)

    def out_group(i, carry):
        chunks = [i * OUT_GROUP + g for g in range(OUT_GROUP)]
        spans = [pl.ds(pl.multiple_of(c * L, L), L) for c in chunks]
        qtb = [qt_ref[0, :, cs] for cs in spans]
        smt = [jnp.dot(k_ref[0, cs, :], q, preferred_element_type=F32)
               for cs, q in zip(spans, qtb)]
        rhs, floor = [], []
        for c, q, sm in zip(chunks, qtb, smt):
            for d, (mask, _, cum_ref, i_off, f_off, _, _) in enumerate(dirs):
                cum = cum_ref[pl.ds((f_off + hd) * nc + c, 1), :]
                r = gi_ref[pl.ds((i_off + hd) * nc + c, 1), :] - cum
                mp = mp_ref[d, pl.ds(c, 1), :]
                r_col = jnp.sum(jnp.where(diag, r, 0.0), axis=1, keepdims=True)
                rm = jnp.where(mask, r_col, NEG)
                cm = jnp.maximum(jnp.max(rm, axis=0, keepdims=True), mp)
                wq = (jnp.exp2(rm - cm) * sm).astype(BF16)
                sq = q * jnp.exp2(mp - cm).astype(BF16)
                rhs.append(jnp.concatenate([wq, sq], axis=0))
                floor.append(jnp.exp2(-(cum + cm)))
        ys = []
        for g, (c, cs) in enumerate(zip(chunks, spans)):
            vat = v_aug(cs)
            for d, (_, c_ref, _, _, _, _, _) in enumerate(dirs):
                ys.append(jnp.dot(jnp.concatenate([vat, c_ref[c]], axis=1), rhs[2 * g + d],
                                  preferred_element_type=F32))
        for g, cs in enumerate(spans):
            hsum = jnp.zeros((M_HEAD_DIM, L), F32)
            for d in range(2):
                y = ys[2 * g + d]
                den = jnp.maximum(jnp.abs(y[M_HEAD_DIM:M_HEAD_DIM + 1, :]), floor[2 * g + d])
                hsum = hsum + y[0:M_HEAD_DIM, :] * (1.0 / den)
            ms = jnp.sum(hsum * hsum, axis=0, keepdims=True) * (1.0 / M_HEAD_DIM)
            hn = hsum * lax.rsqrt(ms + EPS)
            gate = jax.nn.sigmoid(ot_ref[0, :, cs].astype(F32))
            out_ref[:, cs] = (hn * mn_ref[...] * gate).astype(BF16)
        return carry

    lax.fori_loop(0, nc // OUT_GROUP, out_group, 0)


def _mlstm_call(z, zt, g, bg, mn, batch, seq):
    n = batch * seq
    nc = seq // M_CHUNK
    dk = M_HEAD_DIM
    assert seq % (M_CHUNK * OUT_GROUP) == 0

    def tspec(base):
        return pl.BlockSpec((1, LANES, seq), lambda b, h: (base + h, 0, b))

    return pl.pallas_call(
        functools.partial(_mlstm_kernel, seq=seq),
        grid=(batch, M_HEADS),
        in_specs=[
            tspec(TB_MQ),
            pl.BlockSpec((1, seq, LANES), lambda b, h: (ZB_MK + h, b, 0)),
            tspec(TB_MV), tspec(TB_MO),
            pl.BlockSpec((N_GATE, 1, nc, LANES), lambda b, h: (0, b, 0, 0)),
            pl.BlockSpec((N_GATE * nc, LANES), lambda b, h: (0, 0)),
            pl.BlockSpec((dk, LANES), lambda b, h: (h, 0)),
        ],
        out_specs=pl.BlockSpec((dk, seq), lambda b, h: (h, b)),
        out_shape=jax.ShapeDtypeStruct((M_WIDTH, n), BF16),
        scratch_shapes=[
            pltpu.VMEM((nc * N_GATE, LANES), F32),
            pltpu.VMEM((nc * N_GATE, LANES), F32),
            pltpu.VMEM((nc * N_GATE, LANES), F32),
            pltpu.VMEM((nc * N_GATE, LANES), F32),
            pltpu.VMEM((2, nc, LANES), F32),
            pltpu.VMEM((2, nc, LANES), F32),
            pltpu.VMEM((2, nc, LANES), F32),
            pltpu.VMEM((2, nc, LANES), F32),
            pltpu.VMEM((2, nc, LANES), F32),
            pltpu.VMEM((nc, 2 * VA_ROWS, dk), F32),
            pltpu.VMEM((nc, VA_ROWS, dk), BF16),
            pltpu.VMEM((nc, VA_ROWS, dk), BF16),
            pltpu.VMEM((2, VA_ROWS, dk), F32),
        ],
        compiler_params=pltpu.CompilerParams(
            dimension_semantics=("arbitrary", "arbitrary"), vmem_limit_bytes=VMEM_LIMIT_BYTES),
        name="mlstm",
    )(zt, z, zt, zt, g, bg, mn)


def _attn_kernel(sink_ref, qt_ref, kt_ref, vt_ref, cos_ref, sin_ref, qg_ref, kg_ref, out_ref,
                 qs_ref, ks_ref, vs_ref, *, seq):
    W = WINDOW
    nb = seq // W
    kv = pl.program_id(1)
    dh = A_HEAD_DIM
    half = dh // 2

    def swap_halves(a):
        return jnp.concatenate([a[half:], a[:half]], axis=0)

    def rot_gain(g):
        return jnp.concatenate([-g[half:], g[:half]], axis=0)

    q_scale = (dh ** -0.5) * LOG2E
    qg, kg = qg_ref[...] * q_scale, kg_ref[...]
    qg_sw, kg_sw = rot_gain(qg), rot_gain(kg)

    def norm_rope(x, ct, st):
        rinv = lax.rsqrt(jnp.sum(x * x, axis=0, keepdims=True) * (1.0 / dh) + EPS)
        return (x * ct + swap_halves(x) * st) * rinv

    def prep(i, carry):
        ls = pl.ds(pl.multiple_of(i * W, W), W)
        cs = cos_ref[:, ls]
        sn = sin_ref[:, ls]
        qct, qst = cs * qg, sn * qg_sw
        for j in range(2):
            xq = qt_ref[j, :, ls].astype(F32)
            for hh in range(2):
                r0 = (2 * j + hh) * dh
                qs_ref[r0:r0 + dh, ls] = norm_rope(xq[hh * dh:(hh + 1) * dh], qct, qst).astype(BF16)
        xk = kt_ref[0, pl.ds(pl.multiple_of(kv * dh, dh), dh), ls].astype(F32)
        k2 = jnp.concatenate([norm_rope(xk, cs * kg, sn * kg_sw), jnp.zeros((dh, W), F32)], axis=0)
        ks_ref[pl.ds(pl.multiple_of(W + i * W, W), W), :] = k2.T.astype(BF16)
        return carry

    lax.fori_loop(0, nb, prep, 0, unroll=4)
    ks_ref[0:W, :] = jnp.zeros((W, LANES), BF16)
    ks_ref[seq + W:seq + 2 * W, :] = jnp.zeros((W, LANES), BF16)
    vs_ref[0:dh, 0:W] = jnp.zeros((dh, W), BF16)
    vs_ref[0:dh, seq + W:seq + 2 * W] = jnp.zeros((dh, W), BF16)
    vs_ref[0:dh, W:seq + W] = vt_ref[0, pl.ds(pl.multiple_of(kv * dh, dh), dh), :]
    vs_ref[dh:dh + ONES_ROWS, :] = jnp.ones((ONES_ROWS, seq + 2 * W), BF16)

    hpu = ATTN_HEADS_PER_UNIT
    nq = hpu * W
    kj = lax.broadcasted_iota(jnp.int32, (W, nq), 0)
    qi = lax.broadcasted_iota(jnp.int32, (W, nq), 1) % W
    left_band = jnp.where(kj >= qi, 0.0, NEG)
    right_band = jnp.where(kj <= qi, 0.0, NEG)
    head_of_lane = lax.broadcasted_iota(jnp.int32, (1, nq), 1) // W
    sink_rows = []
    for u in range(A_GROUP // hpu):
        row = jnp.zeros((1, nq), F32)
        for hh in range(hpu):
            row = jnp.where(head_of_lane == hh, sink_ref[kv * A_GROUP + u * hpu + hh] * LOG2E, row)
        sink_rows.append(row)

    def blk_group(i, carry):
        units = [(i * ATTN_GROUP + j, u) for j in range(ATTN_GROUP) for u in range(A_GROUP // hpu)]
        qcols = [pl.ds(pl.multiple_of(n * W, W), W) for n, _ in units]
        wins = [pl.ds(pl.multiple_of(n * W, W), 3 * W) for n, _ in units]
        scores = []
        for (n, u), qc, ws in zip(units, qcols, wins):
            zq = jnp.zeros((dh, W), BF16)
            q2 = jnp.concatenate(
                [jnp.concatenate([qs_ref[(u * hpu + hh) * dh:(u * hpu + hh + 1) * dh, qc], zq], axis=0)
                 for hh in range(hpu)], axis=1)
            scores.append(jnp.dot(ks_ref[ws, :], q2, preferred_element_type=F32))
        probs, dens = [], []
        for (n, u), s in zip(units, scores):
            sb = jnp.concatenate([s[0:W] + (left_band + jnp.where(n == 0, NEG, 0.0)),
                                  s[W:2 * W],
                                  s[2 * W:3 * W] + (right_band + jnp.where(n == nb - 1, NEG, 0.0))],
                                 axis=0)
            m = jnp.maximum(jnp.max(sb, axis=0, keepdims=True), sink_rows[u])
            probs.append(jnp.exp2(sb - m).astype(BF16))
            dens.append(jnp.exp2(sink_rows[u] - m))
        pvs = [jnp.dot(vs_ref[:, ws], p, preferred_element_type=F32)
               for ws, p in zip(wins, probs)]
        for (n, u), qc, pv, sink_term in zip(units, qcols, pvs, dens):
            o = pv[0:dh] * (1.0 / (pv[dh:dh + 1] + sink_term))
            for hh in range(hpu):
                r0 = (u * hpu + hh) * dh
                out_ref[r0:r0 + dh, qc] = o[:, hh * W:(hh + 1) * W].astype(BF16)
        return carry

    lax.fori_loop(0, nb // ATTN_GROUP, blk_group, 0)


def _attn_call(zt, sink, cos_t, sin_t, qg, kg, batch, seq):
    n = batch * seq
    qrows = A_GROUP * A_HEAD_DIM
    assert seq % (WINDOW * ATTN_GROUP) == 0 and A_GROUP % ATTN_HEADS_PER_UNIT == 0
    return pl.pallas_call(
        functools.partial(_attn_kernel, seq=seq),
        grid=(batch, A_KV_HEADS),
        in_specs=[
            pl.BlockSpec(memory_space=pltpu.SMEM),
            pl.BlockSpec((2, LANES, seq), lambda b, g: (TB_AQ // 2 + g, 0, b)),
            pl.BlockSpec((1, LANES, seq), lambda b, g: (TB_AK, 0, b)),
            pl.BlockSpec((1, LANES, seq), lambda b, g: (TB_AV, 0, b)),
            pl.BlockSpec((A_HEAD_DIM, seq), lambda b, g: (0, 0)),
            pl.BlockSpec((A_HEAD_DIM, seq), lambda b, g: (0, 0)),
            pl.BlockSpec((A_HEAD_DIM, LANES), lambda b, g: (0, 0)),
            pl.BlockSpec((A_HEAD_DIM, LANES), lambda b, g: (0, 0)),
        ],
        out_specs=pl.BlockSpec((qrows, seq), lambda b, g: (g, b)),
        out_shape=jax.ShapeDtypeStruct((A_WIDTH, n), BF16),
        scratch_shapes=[
            pltpu.VMEM((qrows, seq), BF16),
            pltpu.VMEM((seq + 2 * WINDOW, LANES), BF16),
            pltpu.VMEM((A_HEAD_DIM + ONES_ROWS, seq + 2 * WINDOW), BF16),
        ],
        compiler_params=pltpu.CompilerParams(
            dimension_semantics=("arbitrary", "arbitrary"), vmem_limit_bytes=VMEM_LIMIT_BYTES),
        name="window_attn",
    )(sink, zt, zt, zt, cos_t, sin_t, qg, kg)


def _z_weights(w):
    mq, mk, mv, mo, mg, aq, ak, av, gm = 0, 512, 1024, 1536, 2048, 2064, 2576, 2704, 2832
    parts = [(mq, M_WIDTH), (mv, M_WIDTH), (mo, M_WIDTH), (aq, A_WIDTH),
             (ak, A_KV_HEADS * A_HEAD_DIM), (av, A_KV_HEADS * A_HEAD_DIM),
             (gm, 2 * D_MODEL), (mk, M_WIDTH), (mg, N_GATE)]
    cols = jnp.concatenate([w[:, a:a + n] for a, n in parts], axis=1)
    return jnp.pad(cols, ((0, 0), (0, Z_COLS - cols.shape[1]))).astype(BF16)


def _ffn_weights(w1, w3, w2):
    return w1.astype(BF16), w3.astype(BF16), w2.astype(BF16)


def _rope_tables(seq):
    half = A_HEAD_DIM // 2
    inv = jnp.power(ROPE_THETA, -jnp.arange(half, dtype=F32) / half)
    ang = jnp.arange(seq).astype(F32)[:, None] * inv[None, :]
    return jnp.tile(jnp.cos(ang).T, (2, 1)), jnp.tile(jnp.sin(ang).T, (2, 1))


def _layer(x, p):
    batch, seq, _ = x.shape
    xf = x.reshape(batch * seq, D_MODEL)
    nc = seq // M_CHUNK
    x1, z, zt, gt = _front_call(xf, p["n1"], *p["ffn1"], p["nm"], p["wz"])
    hm = _mlstm_call(z, zt, gt.reshape(N_GATE, batch, nc, M_CHUNK),
                     jnp.broadcast_to(jnp.repeat(p["bg"], nc)[:, None], (N_GATE * nc, LANES)),
                     p["mn"], batch, seq)
    cos_t, sin_t = _rope_tables(seq)
    ha = _attn_call(zt, p["sink"], cos_t, sin_t, p["qn"], p["kn"], batch, seq)
    y = _back_call(x1, z, hm, ha, p["wpm"], p["wpa"], p["wo"], p["n2"], *p["ffn2"])
    return y.reshape(batch, seq, D_MODEL)


def kernel(x_prompt, x_sample, ffn1_norm, ffn1_w1, ffn1_w3, ffn1_w2, mix_norm, w_in, b_gates, m_norm, q_norm, k_norm, sink, w_pm, w_pa, w_out, ffn2_norm, ffn2_w1, ffn2_w3, ffn2_w2):
    depth = w_in.shape[0]
    y_prompt, y_sample = x_prompt, x_sample
    for l in range(depth):
        p = {
            "n1": ffn1_norm[l][None, :],
            "ffn1": _ffn_weights(ffn1_w1[l], ffn1_w3[l], ffn1_w2[l]),
            "nm": mix_norm[l][None, :],
            "wz": _z_weights(w_in[l]),
            "bg": b_gates[l],
            "mn": jnp.broadcast_to(m_norm[l][:, None], (M_WIDTH, LANES)),
            "qn": jnp.broadcast_to(q_norm[l][:, None], (A_HEAD_DIM, LANES)),
            "kn": jnp.broadcast_to(k_norm[l][:, None], (A_HEAD_DIM, LANES)),
            "sink": sink[l],
            "wpm": w_pm[l].astype(BF16),
            "wpa": w_pa[l].astype(BF16),
            "wo": w_out[l].astype(BF16),
            "n2": ffn2_norm[l][None, :],
            "ffn2": _ffn_weights(ffn2_w1[l], ffn2_w3[l], ffn2_w2[l]),
        }
        y_prompt = _layer(y_prompt, p)
        y_sample = _layer(y_sample, p)
    return (y_prompt, y_sample)
```
